```python
import math
import jax, jax.numpy as jnp
from jax import lax
import numpy as np

D_MODEL = 4096
BATCH = 4
SEQ = 2048
DEPTH = 1
DEC_BATCH = 128
DEC_SEQ = 1
PAST_LEN = 16384
PAGE_SIZE = 128

D_MIX = D_MODEL
DA_WIDTH = D_MIX // 2
DA_HEAD_DIM = 64
DA_V_DIM = 2 * DA_HEAD_DIM
DA_HEADS = DA_WIDTH // DA_V_DIM
DA_KV_HEADS = 2
DA_GROUP = DA_HEADS // DA_KV_HEADS
DA_Q_COLS = DA_HEADS * 2 * DA_HEAD_DIM
DA_K_COLS = DA_KV_HEADS * 2 * DA_HEAD_DIM
DA_V_COLS = DA_KV_HEADS * DA_V_DIM
DA_SCALE = DA_HEAD_DIM ** -0.5
MLA_WIDTH = D_MIX - DA_WIDTH
MLA_V_DIM = 128
MLA_HEADS = MLA_WIDTH // MLA_V_DIM
MLA_NOPE_DIM = 128
MLA_ROPE_DIM = 64
MLA_KV_RANK = 512
MLA_Q_RANK = 1024
MLA_SCALE = (MLA_NOPE_DIM + MLA_ROPE_DIM) ** -0.5
N_IN = DA_Q_COLS + DA_K_COLS + DA_V_COLS + DA_WIDTH + MLA_Q_RANK + MLA_KV_RANK + MLA_ROPE_DIM + MLA_WIDTH
ROPE_THETA = 10000.0
EPS = 1e-6
Q_BLOCK = 128
NEG_INF = -1e30

kernel_name = 'hybrid_diffattn_mla_decode_step'


def _rms_norm(x, g):
    xf = x.astype(jnp.float32)
    y = xf * lax.rsqrt(jnp.mean(xf * xf, axis=-1, keepdims=True) + EPS)
    return (y * g.astype(jnp.float32)).astype(x.dtype)


def _rope(x, pos):
    d = x.shape[-1]
    inv = jnp.power(ROPE_THETA, -jnp.arange(0, d, 2, dtype=jnp.float32) / d)
    ang = pos.astype(jnp.float32)[:, None] * inv[None, :]
    ang = jnp.concatenate([ang, ang], axis=-1)
    shape = (pos.shape[0],) + (1,) * (x.ndim - 3) + (d,)
    cos = jnp.cos(ang).reshape(shape)
    sin = jnp.sin(ang).reshape(shape)
    rot = jnp.concatenate([-x[..., d // 2:], x[..., :d // 2]], axis=-1)
    return (x.astype(jnp.float32) * cos + rot.astype(jnp.float32) * sin).astype(x.dtype)


def _split_in(z):
    sizes = (DA_Q_COLS, DA_K_COLS, DA_V_COLS, DA_WIDTH, MLA_Q_RANK, MLA_KV_RANK + MLA_ROPE_DIM, MLA_WIDTH)
    return jnp.split(z, np.cumsum(sizes)[:-1].tolist(), axis=-1)


def _da_lambda(lq1, lk1, lq2, lk2, lam_init):
    f = lambda a, b: jnp.exp(jnp.sum(a.astype(jnp.float32) * b.astype(jnp.float32)))
    return f(lq1, lk1) - f(lq2, lk2) + lam_init


def _project(x, pos, g_pre, w_in, da_q_norm, da_k_norm, mla_q_a_norm, w_uq, mla_qn_norm,
             mla_qpe_norm, mla_kv_a_norm, mla_kpe_norm, w_uk):
    B, T, _ = x.shape
    h = _rms_norm(x, g_pre)
    z = jnp.einsum('btd,dn->btn', h, w_in)
    zq, zk, zv, gate_a, zqa, zkva, gate_b = _split_in(z)
    q = _rope(_rms_norm(zq.reshape(B, T, DA_KV_HEADS, DA_GROUP, 2, DA_HEAD_DIM), da_q_norm), pos)
    k = _rope(_rms_norm(zk.reshape(B, T, DA_KV_HEADS, 2, DA_HEAD_DIM), da_k_norm), pos)
    v = zv.reshape(B, T, DA_KV_HEADS, DA_V_DIM)
    qm = jnp.einsum('btr,rhe->bthe', _rms_norm(zqa, mla_q_a_norm), w_uq)
    q_nope = _rms_norm(qm[..., :MLA_NOPE_DIM], mla_qn_norm)
    q_pe = _rope(_rms_norm(qm[..., MLA_NOPE_DIM:], mla_qpe_norm), pos)
    q_lat = jnp.einsum('bthn,chn->bthc', q_nope, w_uk)
    ckv = _rms_norm(zkva[..., :MLA_KV_RANK], mla_kv_a_norm)
    k_pe = _rope(_rms_norm(zkva[..., MLA_KV_RANK:], mla_kpe_norm), pos)
    return q, k, v, gate_a, q_lat, q_pe, ckv, k_pe, gate_b


def _da_scores(q, k):
    return jnp.einsum('bqgrcd,bkgcd->bgrcqk', q, k, preferred_element_type=jnp.float32) * DA_SCALE


def _mla_scores(q_lat, q_pe, ckv, k_pe):
    s = (jnp.einsum('bqhc,bkc->bhqk', q_lat, ckv, preferred_element_type=jnp.float32)
         + jnp.einsum('bqhr,bkr->bhqk', q_pe, k_pe, preferred_element_type=jnp.float32))
    return s * MLA_SCALE


def _online_step(carry, s, pv):
    m, l, acc = carry
    m_new = jnp.maximum(m, jnp.max(s, axis=-1))
    corr = jnp.exp(m - m_new)
    p = jnp.exp(s - m_new[..., None])
    return (m_new, l * corr + jnp.sum(p, axis=-1), acc * corr[..., None] + pv(p))


def _da_prompt(q, k, v, lam):
    B, S = q.shape[:2]
    nb = S // Q_BLOCK
    qb = jnp.moveaxis(q.reshape((B, nb, Q_BLOCK) + q.shape[2:]), 1, 0)
    kpos = jnp.arange(S)
    vf = v.astype(jnp.float32)

    def block(args):
        qi, i = args
        qpos = i * Q_BLOCK + jnp.arange(Q_BLOCK)
        s = jnp.where(qpos[:, None] >= kpos[None, :], _da_scores(qi, k), NEG_INF)
        p = jax.nn.softmax(s, axis=-1)
        a = p[:, :, :, 0] - lam * p[:, :, :, 1]
        return jnp.einsum('bgrqk,bkgv->bqgrv', a, vf)

    o = lax.map(block, (qb, jnp.arange(nb)))
    return jnp.moveaxis(o, 0, 1).reshape(B, S, DA_HEADS, DA_V_DIM)


def _mla_prompt(q_lat, q_pe, ckv, k_pe):
    B, S = q_lat.shape[:2]
    nb = S // Q_BLOCK
    blocks = lambda t: jnp.moveaxis(t.reshape((B, nb, Q_BLOCK) + t.shape[2:]), 1, 0)
    kpos = jnp.arange(S)
    cf = ckv.astype(jnp.float32)

    def block(args):
        ql, qp, i = args
        qpos = i * Q_BLOCK + jnp.arange(Q_BLOCK)
        s = jnp.where(qpos[:, None] >= kpos[None, :], _mla_scores(ql, qp, ckv, k_pe), NEG_INF)
        p = jax.nn.softmax(s, axis=-1)
        return jnp.einsum('bhqk,bkc->bqhc', p, cf)

    o = lax.map(block, (blocks(q_lat), blocks(q_pe), jnp.arange(nb)))
    return jnp.moveaxis(o, 0, 1).reshape(B, S, MLA_HEADS, MLA_KV_RANK)


def _da_sample(q, k, v, cache_k, cache_v, page_table, layer, lam):
    Bd, T = q.shape[:2]
    page = cache_k.shape[2]
    stat = (Bd, DA_KV_HEADS, DA_GROUP, 2, T)
    init = (jnp.full(stat, NEG_INF, jnp.float32), jnp.zeros(stat, jnp.float32),
            jnp.zeros(stat + (DA_V_DIM,), jnp.float32))

    def step(carry, phys):
        kk = cache_k[layer, phys].reshape(Bd, page, DA_KV_HEADS, 2, DA_HEAD_DIM)
        vv = cache_v[layer, phys].astype(jnp.float32)
        pv = lambda p: jnp.einsum('bgrcqk,bkgv->bgrcqv', p, vv)
        return _online_step(carry, _da_scores(q, kk), pv), None

    carry, _ = lax.scan(step, init, page_table.T)
    causal = jnp.tril(jnp.ones((T, T), dtype=bool))
    s = jnp.where(causal, _da_scores(q, k), NEG_INF)
    vf = v.astype(jnp.float32)
    _, l, acc = _online_step(carry, s, lambda p: jnp.einsum('bgrcqk,bkgv->bgrcqv', p, vf))
    o = acc / l[..., None]
    o = o[:, :, :, 0] - lam * o[:, :, :, 1]
    return o.transpose(0, 3, 1, 2, 4).reshape(Bd, T, DA_HEADS, DA_V_DIM)


def _mla_sample(q_lat, q_pe, ckv, k_pe, cache_ckv, cache_kpe, page_table, layer):
    Bd, T = q_lat.shape[:2]
    stat = (Bd, MLA_HEADS, T)
    init = (jnp.full(stat, NEG_INF, jnp.float32), jnp.zeros(stat, jnp.float32),
            jnp.zeros(stat + (MLA_KV_RANK,), jnp.float32))

    def step(carry, phys):
        cc = cache_ckv[layer, phys].astype(jnp.float32)
        kp = cache_kpe[layer, phys]
        pv = lambda p: jnp.einsum('bhqk,bkc->bhqc', p, cc)
        return _online_step(carry, _mla_scores(q_lat, q_pe, cc, kp), pv), None

    carry, _ = lax.scan(step, init, page_table.T)
    causal = jnp.tril(jnp.ones((T, T), dtype=bool))
    s = jnp.where(causal, _mla_scores(q_lat, q_pe, ckv, k_pe), NEG_INF)
    cf = ckv.astype(jnp.float32)
    _, l, acc = _online_step(carry, s, lambda p: jnp.einsum('bhqk,bkc->bhqc', p, cf))
    return jnp.moveaxis(acc / l[..., None], 1, 2)


def _output(x, o_da, o_lat, gate_a, gate_b, lam_init, da_subln, w_uv, w_out):
    B, T, _ = x.shape
    o_a = _rms_norm(o_da.astype(x.dtype), da_subln) * (1.0 - lam_init)
    o_a = o_a.reshape(B, T, DA_WIDTH) * jax.nn.silu(gate_a)
    o_b = jnp.einsum('bthc,chv->bthv', o_lat.astype(x.dtype), w_uv).reshape(B, T, MLA_WIDTH)
    o_b = o_b * jax.nn.silu(gate_b)
    return x + jnp.einsum('btm,md->btd', jnp.concatenate([o_a, o_b], axis=-1), w_out)


def setup_inputs(seed: int = 0) -> dict:
    key = jax.random.key(seed)
    ks = jax.random.split(key, 32)
    n_pages = PAST_LEN // PAGE_SIZE
    n_used = DEC_BATCH * n_pages
    n_phys = n_used + max(1, n_used // 4)
    nrm = lambda k, shape, scale: scale * jax.random.normal(k, shape, jnp.float32)
    gain = lambda k, n: 1.0 + 0.02 * jax.random.normal(k, (DEPTH, n), jnp.float32)
    page_table = jax.random.permutation(ks[6], n_phys)[:n_used].reshape(DEC_BATCH, n_pages).astype(jnp.int32)
    return {
        'x_prompt': nrm(ks[0], (BATCH, SEQ, D_MODEL), 1.0),
        'x_sample': nrm(ks[1], (DEC_BATCH, DEC_SEQ, D_MODEL), 1.0),
        'cache_diff_k': nrm(ks[2], (DEPTH, n_phys, PAGE_SIZE, DA_KV_HEADS, 2 * DA_HEAD_DIM), 1.0),
        'cache_diff_v': nrm(ks[3], (DEPTH, n_phys, PAGE_SIZE, DA_KV_HEADS, DA_V_DIM), 1.0),
        'cache_mla_ckv': nrm(ks[4], (DEPTH, n_phys, PAGE_SIZE, MLA_KV_RANK), 1.0),
        'cache_mla_kpe': nrm(ks[5], (DEPTH, n_phys, PAGE_SIZE, MLA_ROPE_DIM), 1.0),
        'page_table': page_table,
        'g_pre': gain(ks[7], D_MODEL),
        'w_in': nrm(ks[8], (DEPTH, D_MODEL, N_IN), D_MODEL ** -0.5),
        'da_q_norm': gain(ks[9], DA_HEAD_DIM),
        'da_k_norm': gain(ks[10], DA_HEAD_DIM),
        'da_lambda_q1': nrm(ks[11], (DEPTH, DA_HEAD_DIM), 0.1),
        'da_lambda_k1': nrm(ks[12], (DEPTH, DA_HEAD_DIM), 0.1),
        'da_lambda_q2': nrm(ks[13], (DEPTH, DA_HEAD_DIM), 0.1),
        'da_lambda_k2': nrm(ks[14], (DEPTH, DA_HEAD_DIM), 0.1),
        'da_subln': gain(ks[15], DA_V_DIM),
        'mla_q_a_norm': gain(ks[16], MLA_Q_RANK),
        'w_uq': nrm(ks[17], (DEPTH, MLA_Q_RANK, MLA_HEADS, MLA_NOPE_DIM + MLA_ROPE_DIM), MLA_Q_RANK ** -0.5),
        'mla_qn_norm': gain(ks[18], MLA_NOPE_DIM),
        'mla_qpe_norm': gain(ks[19], MLA_ROPE_DIM),
        'mla_kv_a_norm': gain(ks[20], MLA_KV_RANK),
        'mla_kpe_norm': gain(ks[21], MLA_ROPE_DIM),
        'w_uk': nrm(ks[22], (DEPTH, MLA_KV_RANK, MLA_HEADS, MLA_NOPE_DIM), MLA_KV_RANK ** -0.5),
        'w_uv': nrm(ks[23], (DEPTH, MLA_KV_RANK, MLA_HEADS, MLA_V_DIM), MLA_KV_RANK ** -0.5),
        'w_out': nrm(ks[24], (DEPTH, D_MIX, D_MODEL), D_MIX ** -0.5),
    }


def reference(x_prompt, x_sample, cache_diff_k, cache_diff_v, cache_mla_ckv, cache_mla_kpe, page_table,
              g_pre, w_in, da_q_norm, da_k_norm, da_lambda_q1, da_lambda_k1, da_lambda_q2, da_lambda_k2,
              da_subln, mla_q_a_norm, w_uq, mla_qn_norm, mla_qpe_norm, mla_kv_a_norm, mla_kpe_norm,
              w_uk, w_uv, w_out):
    B, S, _ = x_prompt.shape
    Bd, T, _ = x_sample.shape
    past_len = page_table.shape[1] * cache_mla_ckv.shape[2]
    pos_p = jnp.arange(S)
    pos_s = past_len + jnp.arange(T)
    hp, hs = x_prompt, x_sample
    rows_p, rows_s = [], []
    for l in range(DEPTH):
        lam_init = 0.8 - 0.6 * math.exp(-0.3 * l)
        lam = _da_lambda(da_lambda_q1[l], da_lambda_k1[l], da_lambda_q2[l], da_lambda_k2[l], lam_init)
        proj_w = (g_pre[l], w_in[l], da_q_norm[l], da_k_norm[l], mla_q_a_norm[l], w_uq[l], mla_qn_norm[l],
                  mla_qpe_norm[l], mla_kv_a_norm[l], mla_kpe_norm[l], w_uk[l])
        q, k, v, ga, ql, qpe, ckv, kpe, gb = _project(hp, pos_p, *proj_w)
        o_da = _da_prompt(q, k, v, lam)
        o_lat = _mla_prompt(ql, qpe, ckv, kpe)
        rows_p.append((k.reshape(B, S, DA_KV_HEADS, 2 * DA_HEAD_DIM), v, ckv, kpe))
        hp = _output(hp, o_da, o_lat, ga, gb, lam_init, da_subln[l], w_uv[l], w_out[l])
        q, k, v, ga, ql, qpe, ckv, kpe, gb = _project(hs, pos_s, *proj_w)
        o_da = _da_sample(q, k, v, cache_diff_k, cache_diff_v, page_table, l, lam)
        o_lat = _mla_sample(ql, qpe, ckv, kpe, cache_mla_ckv, cache_mla_kpe, page_table, l)
        rows_s.append((k.reshape(Bd, T, DA_KV_HEADS, 2 * DA_HEAD_DIM), v, ckv, kpe))
        hs = _output(hs, o_da, o_lat, ga, gb, lam_init, da_subln[l], w_uv[l], w_out[l])
    new_dk_p = jnp.stack([r[0] for r in rows_p])
    new_dv_p = jnp.stack([r[1] for r in rows_p])
    new_ckv_p = jnp.stack([r[2] for r in rows_p])
    new_kpe_p = jnp.stack([r[3] for r in rows_p])
    new_dk_s = jnp.stack([r[0] for r in rows_s])
    new_dv_s = jnp.stack([r[1] for r in rows_s])
    new_ckv_s = jnp.stack([r[2] for r in rows_s])
    new_kpe_s = jnp.stack([r[3] for r in rows_s])
    return (hp, hs, new_dk_p, new_dv_p, new_ckv_p, new_kpe_p, new_dk_s, new_dv_s, new_ckv_s, new_kpe_s)
```

```python
import functools
import math

import jax
import jax.numpy as jnp
from jax import lax
from jax.experimental import pallas as pl
from jax.experimental.pallas import tpu as pltpu

F32 = jnp.float32
BF16 = jnp.bfloat16

EPS = 1e-6
ROPE_THETA = 10000.0
NEG_INF = -1e30
LANES = 128
MIB = 1024 * 1024

DA_HEAD_DIM = 64
DA_V_DIM = 128
DA_HEADS = 16
DA_KV_HEADS = 2
DA_GROUP = DA_HEADS // DA_KV_HEADS
DA_WIDTH = DA_HEADS * DA_V_DIM
DA_SCALE = DA_HEAD_DIM ** -0.5
MLA_HEADS = 16
MLA_NOPE = 128
MLA_ROPE = 64
MLA_V = 128
MLA_KV_RANK = 512
MLA_Q_RANK = 1024
MLA_WIDTH = MLA_HEADS * MLA_V
MLA_SCALE = (MLA_NOPE + MLA_ROPE) ** -0.5
MLA_QK = 2 * LANES
LAM_INIT = 0.8 - 0.6 * math.exp(-0.3 * 0)

COL_Q = 0
COL_GA = COL_Q + DA_HEADS * 2 * DA_HEAD_DIM
COL_GB = COL_GA + DA_WIDTH
COL_QA = COL_GB + MLA_WIDTH
COL_CKV = COL_QA + MLA_Q_RANK
COL_KV = COL_CKV + MLA_KV_RANK
COL_KPE = COL_KV + 2 * DA_KV_HEADS * DA_V_DIM
INPROJ_TN = 768
N_PAD = COL_KPE + 256
assert N_PAD % INPROJ_TN == 0


def _params(sem, vmem_mib):
    return pltpu.CompilerParams(dimension_semantics=sem, vmem_limit_bytes=vmem_mib * MIB)


def _silu(g):
    return g / (1.0 + jnp.exp(-g))


def _inproj_body(x_ref, g_ref, w_ref, z_ref, h_ref):
    @pl.when(pl.program_id(1) == 0)
    def _():
        x = x_ref[...]
        ms = jnp.mean(x * x, axis=-1, keepdims=True)
        h_ref[...] = (x * lax.rsqrt(ms + EPS) * g_ref[...]).astype(BF16)

    z_ref[...] = jnp.dot(h_ref[...], w_ref[...], preferred_element_type=F32)


def _inproj(x, g, w, tm):
    T, D = x.shape
    N = w.shape[1]
    tn = INPROJ_TN
    return pl.pallas_call(
        _inproj_body,
        grid=(T // tm, N // tn),
        in_specs=[pl.BlockSpec((tm, D), lambda i, j: (i, 0)),
                  pl.BlockSpec((1, D), lambda i, j: (0, 0)),
                  pl.BlockSpec((D, tn), lambda i, j: (0, j))],
        out_specs=pl.BlockSpec((tm, tn), lambda i, j: (i, j)),
        out_shape=jax.ShapeDtypeStruct((T, N), F32),
        scratch_shapes=[pltpu.VMEM((tm, D), BF16)],
        compiler_params=_params(("parallel", "arbitrary"), 48),
        name="inproj",
    )(x, g, w)


def _chunk_mean_sq(x, m):
    s = x * x
    hi = s.astype(BF16)
    lo = (s - hi.astype(F32)).astype(BF16)
    return (jnp.dot(hi, m, preferred_element_type=F32)
            + jnp.dot(lo, m, preferred_element_type=F32))


def _norm_rope(x, m, gain, cos, sa, sb):
    y = x * lax.rsqrt(_chunk_mean_sq(x, m) + EPS) * gain
    return y * cos + pltpu.roll(y, 96, 1) * sa + pltpu.roll(y, 32, 1) * sb


def _rope_tables(pos, dual):
    d = DA_HEAD_DIM
    inv = jnp.power(ROPE_THETA, -jnp.arange(0, d, 2, dtype=F32) / d)
    ang = pos.astype(F32)[:, None] * inv[None, :]
    ang = jnp.concatenate([ang, ang], axis=-1)
    cos, sin = jnp.cos(ang), jnp.sin(ang)
    low = jnp.arange(d) < d // 2
    sa = jnp.where(low, -sin, 0.0)
    sb = jnp.where(low, 0.0, sin)
    if dual:
        return tuple(jnp.concatenate([t, t], axis=-1) for t in (cos, sa, sb))
    return tuple(jnp.concatenate([t, jnp.zeros_like(t)], axis=-1) for t in (cos, sa, sb))


def _avg_matrix(chunk, rows):
    i = jnp.arange(LANES)[:, None]
    j = jnp.arange(LANES)[None, :]
    same = (i // chunk == j // chunk) if rows == LANES else (i < rows) & (j >= 0)
    return jnp.where(same, 1.0 / chunk, 0.0).astype(BF16)


def _da_prep_body(zq_ref, zkv_ref, m_ref, gq_ref, gk_ref, cos_ref, sa_ref, sb_ref,
                  q_ref, k_ref, v_ref, kb_ref, vb_ref):
    m = m_ref[...]
    cos, sa, sb = cos_ref[...], sa_ref[...], sb_ref[...]
    gq, gk = gq_ref[...], gk_ref[...]
    for c in range(DA_HEADS):
        sl = slice(c * LANES, (c + 1) * LANES)
        q_ref[:, sl] = (_norm_rope(zq_ref[:, sl], m, gq, cos, sa, sb) * DA_SCALE).astype(BF16)
    for c in range(DA_KV_HEADS):
        sl = slice(c * LANES, (c + 1) * LANES)
        k = _norm_rope(zkv_ref[:, sl], m, gk, cos, sa, sb)
        k_ref[:, sl] = k
        kb_ref[:, sl] = k.astype(BF16)
    v = zkv_ref[:, DA_KV_HEADS * LANES:]
    v_ref[...] = v
    vb_ref[...] = v.astype(BF16)


def _da_prep(z, gq, gk, tables, tm, n_pos_blocks):
    T = z.shape[0]
    qw = DA_HEADS * LANES
    kw = DA_KV_HEADS * LANES
    row = lambda i: (i, 0)
    const = lambda i: (0, 0)
    tab = lambda i: (i % n_pos_blocks, 0)
    return pl.pallas_call(
        _da_prep_body,
        grid=(T // tm,),
        in_specs=[pl.BlockSpec((tm, qw), lambda i: (i, COL_Q // qw)),
                  pl.BlockSpec((tm, 2 * kw), lambda i: (i, COL_KV // (2 * kw))),
                  pl.BlockSpec((LANES, LANES), const),
                  pl.BlockSpec((1, LANES), const),
                  pl.BlockSpec((1, LANES), const),
                  pl.BlockSpec((tm, LANES), tab),
                  pl.BlockSpec((tm, LANES), tab),
                  pl.BlockSpec((tm, LANES), tab)],
        out_specs=[pl.BlockSpec((tm, qw), row),
                   pl.BlockSpec((tm, kw), row),
                   pl.BlockSpec((tm, kw), row),
                   pl.BlockSpec((tm, kw), row),
                   pl.BlockSpec((tm, kw), row)],
        out_shape=[jax.ShapeDtypeStruct((T, qw), BF16),
                   jax.ShapeDtypeStruct((T, kw), F32),
                   jax.ShapeDtypeStruct((T, kw), F32),
                   jax.ShapeDtypeStruct((T, kw), BF16),
                   jax.ShapeDtypeStruct((T, kw), BF16)],
        compiler_params=_params(("parallel",), 40),
        name="da_prep",
    )(z, z, _avg_matrix(DA_HEAD_DIM, LANES), gq, gk, *tables)


def _mla_prep_body(prompt, zqa_ref, zckv_ref, zkpe_ref, gqa_ref, wuq_ref, m128_ref, m64_ref,
                   gqn_ref, gqpe_ref, gckv_ref, gkpe_ref, cos_ref, sa_ref, sb_ref, wx_ref,
                   ckv_ref, kpe_ref, *outs):
    m128, m64 = m128_ref[...], m64_ref[...]
    cos, sa, sb = cos_ref[...], sa_ref[...], sb_ref[...]
    gqn, gqpe = gqn_ref[...], gqpe_ref[...]

    xa = zqa_ref[...]
    qa = (xa * lax.rsqrt(jnp.mean(xa * xa, axis=-1, keepdims=True) + EPS) * gqa_ref[...]).astype(BF16)
    xc = zckv_ref[...]
    ckv = xc * lax.rsqrt(jnp.mean(xc * xc, axis=-1, keepdims=True) + EPS) * gckv_ref[...]
    kpe = _norm_rope(zkpe_ref[...], m64, gkpe_ref[...], cos, sa, sb)
    ckv_ref[...] = ckv
    kpe_ref[...] = kpe[:, :MLA_ROPE]

    for h in range(MLA_HEADS):
        qm = jnp.dot(qa, wuq_ref[:, h * MLA_QK:(h + 1) * MLA_QK], preferred_element_type=F32)
        qn = qm[:, :LANES]
        qn = qn * lax.rsqrt(_chunk_mean_sq(qn, m128) + EPS) * gqn
        qp = _norm_rope(qm[:, LANES:], m64, gqpe, cos, sa, sb)
        if prompt:
            q_ref = outs[0]
            q_ref[:, h * MLA_QK:h * MLA_QK + LANES] = (qn * MLA_SCALE).astype(BF16)
            q_ref[:, h * MLA_QK + LANES:(h + 1) * MLA_QK] = (qp * MLA_SCALE).astype(BF16)
        else:
            qlat_ref, qpe_ref = outs
            qlat = jnp.dot(qn.astype(BF16), wx_ref[h], preferred_element_type=F32)
            qlat_ref[:, h * MLA_KV_RANK:(h + 1) * MLA_KV_RANK] = (qlat * MLA_SCALE).astype(BF16)
            qpe_ref[:, h * LANES:(h + 1) * LANES] = (qp * MLA_SCALE).astype(BF16)

    if prompt:
        _, k_ref, v_ref = outs
        kv = jnp.dot(ckv.astype(BF16), wx_ref[...], preferred_element_type=F32)
        kpb = kpe.astype(BF16)
        for h in range(MLA_HEADS):
            k_ref[:, h * MLA_QK:h * MLA_QK + LANES] = kv[:, h * LANES:(h + 1) * LANES].astype(BF16)
            k_ref[:, h * MLA_QK + LANES:(h + 1) * MLA_QK] = kpb
        v_ref[...] = kv[:, MLA_HEADS * MLA_NOPE:].astype(BF16)


def _mla_prep(z, prompt, gqa, wuq, gqn, gqpe, gckv, gkpe, tables, wx, tm, n_pos_blocks):
    T = z.shape[0]
    row = lambda i: (i, 0)
    const = lambda i: (0, 0)
    tab = lambda i: (i % n_pos_blocks, 0)
    if prompt:
        wx_spec = pl.BlockSpec(wx.shape, const)
        extra_specs = [pl.BlockSpec((tm, MLA_HEADS * MLA_QK), row),
                       pl.BlockSpec((tm, MLA_HEADS * MLA_QK), row),
                       pl.BlockSpec((tm, MLA_WIDTH), row)]
        extra_shapes = [jax.ShapeDtypeStruct((T, MLA_HEADS * MLA_QK), BF16),
                        jax.ShapeDtypeStruct((T, MLA_HEADS * MLA_QK), BF16),
                        jax.ShapeDtypeStruct((T, MLA_WIDTH), BF16)]
    else:
        wx_spec = pl.BlockSpec(wx.shape, lambda i: (0, 0, 0))
        extra_specs = [pl.BlockSpec((tm, MLA_HEADS * MLA_KV_RANK), row),
                       pl.BlockSpec((tm, MLA_HEADS * LANES), row)]
        extra_shapes = [jax.ShapeDtypeStruct((T, MLA_HEADS * MLA_KV_RANK), BF16),
                        jax.ShapeDtypeStruct((T, MLA_HEADS * LANES), BF16)]
    return pl.pallas_call(
        functools.partial(_mla_prep_body, prompt),
        grid=(T // tm,),
        in_specs=[pl.BlockSpec((tm, MLA_Q_RANK), lambda i: (i, COL_QA // MLA_Q_RANK)),
                  pl.BlockSpec((tm, MLA_KV_RANK), lambda i: (i, COL_CKV // MLA_KV_RANK)),
                  pl.BlockSpec((tm, LANES), lambda i: (i, COL_KPE // LANES)),
                  pl.BlockSpec((1, MLA_Q_RANK), const),
                  pl.BlockSpec(wuq.shape, const),
                  pl.BlockSpec((LANES, LANES), const),
                  pl.BlockSpec((LANES, LANES), const),
                  pl.BlockSpec((1, LANES), const),
                  pl.BlockSpec((1, LANES), const),
                  pl.BlockSpec((1, MLA_KV_RANK), const),
                  pl.BlockSpec((1, LANES), const),
                  pl.BlockSpec((tm, LANES), tab),
                  pl.BlockSpec((tm, LANES), tab),
                  pl.BlockSpec((tm, LANES), tab),
                  wx_spec],
        out_specs=[pl.BlockSpec((tm, MLA_KV_RANK), row),
                   pl.BlockSpec((tm, MLA_ROPE), row)] + extra_specs,
        out_shape=[jax.ShapeDtypeStruct((T, MLA_KV_RANK), F32),
                   jax.ShapeDtypeStruct((T, MLA_ROPE), F32)] + extra_shapes,
        compiler_params=_params(("parallel",), 56),
        name="mla_prep_prompt" if prompt else "mla_prep_sample",
    )(z, z, z, gqa, wuq, _avg_matrix(LANES, LANES), _avg_matrix(MLA_ROPE, MLA_ROPE),
      gqn, gqpe, gckv, gkpe, *tables, wx)


def _causal_mask(t):
    r = lax.broadcasted_iota(jnp.int32, (t, t), 0)
    c = lax.broadcasted_iota(jnp.int32, (t, t), 1)
    return r >= c


def _da_lambda(lv):
    e1 = jnp.exp(jnp.sum(lv[0:1] * lv[1:2], axis=-1, keepdims=True))
    e2 = jnp.exp(jnp.sum(lv[2:3] * lv[3:4], axis=-1, keepdims=True))
    return e1 - e2 + LAM_INIT


def _da_attn_body(q_ref, k_ref, v_ref, g_ref, lv_ref, gs_ref, o_ref, qbd_ref, m_ref, l_ref, acc_ref, *, t):
    qi = pl.program_id(2)
    rows = 2 * DA_GROUP
    low = lax.broadcasted_iota(jnp.int32, (1, LANES), 1) < DA_HEAD_DIM
    for r in range(DA_GROUP):
        q = q_ref[:, r * LANES:(r + 1) * LANES]
        zero = jnp.zeros_like(q)
        qbd_ref[(2 * r) * t:(2 * r + 1) * t, :] = jnp.where(low, q, zero)
        qbd_ref[(2 * r + 1) * t:(2 * r + 2) * t, :] = jnp.where(low, zero, q)
    m_ref[...] = jnp.full(m_ref.shape, NEG_INF, F32)
    l_ref[...] = jnp.zeros(l_ref.shape, F32)
    acc_ref[...] = jnp.zeros(acc_ref.shape, F32)

    def step(ki, masked):
        k = k_ref[pl.ds(ki * t, t), :]
        v = v_ref[pl.ds(ki * t, t), :]
        s = lax.dot_general(qbd_ref[...], k, (((1,), (1,)), ((), ())), preferred_element_type=F32)
        if masked:
            s = jnp.where(_causal_mask(t)[None], s.reshape(rows, t, t), NEG_INF).reshape(rows * t, t)
        m_prev = m_ref[...]
        m_new = jnp.maximum(m_prev, jnp.max(s, axis=-1, keepdims=True))
        corr = jnp.exp(m_prev - m_new)
        p = jnp.exp(s - m_new)
        l_ref[...] = l_ref[...] * corr + jnp.sum(p, axis=-1, keepdims=True)
        acc_ref[...] = acc_ref[...] * corr + jnp.dot(p.astype(BF16), v, preferred_element_type=F32)
        m_ref[...] = m_new

    def loop_body(ki, carry):
        step(ki, False)
        return carry

    lax.fori_loop(0, qi, loop_body, 0)
    step(qi, True)

    lam = _da_lambda(lv_ref[...])
    gs = gs_ref[...]
    for r in range(DA_GROUP):
        a1 = slice((2 * r) * t, (2 * r + 1) * t)
        a2 = slice((2 * r + 1) * t, (2 * r + 2) * t)
        od = acc_ref[a1, :] / l_ref[a1, :] - lam * (acc_ref[a2, :] / l_ref[a2, :])
        y = od * lax.rsqrt(jnp.mean(od * od, axis=-1, keepdims=True) + EPS) * gs * (1.0 - LAM_INIT)
        sl = slice(r * LANES, (r + 1) * LANES)
        o_ref[:, sl] = (y * _silu(g_ref[:, sl])).astype(BF16)


def _da_attn(q, kb, vb, z, lv, gs, B, S, t):
    T = q.shape[0]
    nq = S // t
    gw = DA_GROUP * LANES
    rows = 2 * DA_GROUP * t
    qmap = lambda b, g, i: (b * nq + i, g)
    return pl.pallas_call(
        functools.partial(_da_attn_body, t=t),
        grid=(B, DA_KV_HEADS, nq),
        in_specs=[pl.BlockSpec((t, gw), qmap),
                  pl.BlockSpec((S, LANES), lambda b, g, i: (b, g)),
                  pl.BlockSpec((S, LANES), lambda b, g, i: (b, g)),
                  pl.BlockSpec((t, gw), lambda b, g, i: (b * nq + i, COL_GA // gw + g)),
                  pl.BlockSpec((4, DA_HEAD_DIM), lambda b, g, i: (0, 0)),
                  pl.BlockSpec((1, LANES), lambda b, g, i: (0, 0))],
        out_specs=pl.BlockSpec((t, gw), qmap),
        out_shape=jax.ShapeDtypeStruct((T, DA_WIDTH), BF16),
        scratch_shapes=[pltpu.VMEM((rows, LANES), BF16),
                        pltpu.VMEM((rows, 1), F32),
                        pltpu.VMEM((rows, 1), F32),
                        pltpu.VMEM((rows, LANES), F32)],
        compiler_params=_params(("parallel", "parallel", "arbitrary"), 48),
        name="da_attn",
    )(q, kb, vb, z, lv, gs)


def _mla_attn_body(q_ref, k_ref, v_ref, g_ref, o_ref, m_ref, l_ref, acc_ref, *, t, hb):
    qi = pl.program_id(2)
    m_ref[...] = jnp.full(m_ref.shape, NEG_INF, F32)
    l_ref[...] = jnp.zeros(l_ref.shape, F32)
    acc_ref[...] = jnp.zeros(acc_ref.shape, F32)

    def step(ki, masked):
        for h in range(hb):
            q = q_ref[:, h * MLA_QK:(h + 1) * MLA_QK]
            k = k_ref[pl.ds(ki * t, t), h * MLA_QK:(h + 1) * MLA_QK]
            v = v_ref[pl.ds(ki * t, t), h * MLA_V:(h + 1) * MLA_V]
            s = lax.dot_general(q, k, (((1,), (1,)), ((), ())), preferred_element_type=F32)
            if masked:
                s = jnp.where(_causal_mask(t), s, NEG_INF)
            m_prev = m_ref[h]
            m_new = jnp.maximum(m_prev, jnp.max(s, axis=-1, keepdims=True))
            corr = jnp.exp(m_prev - m_new)
            p = jnp.exp(s - m_new)
            l_ref[h] = l_ref[h] * corr + jnp.sum(p, axis=-1, keepdims=True)
            acc_ref[h] = acc_ref[h] * corr + jnp.dot(p.astype(BF16), v, preferred_element_type=F32)
            m_ref[h] = m_new

    def loop_body(ki, carry):
        step(ki, False)
        return carry

    lax.fori_loop(0, qi, loop_body, 0)
    step(qi, True)

    for h in range(hb):
        sl = slice(h * MLA_V, (h + 1) * MLA_V)
        o_ref[:, sl] = (acc_ref[h] / l_ref[h] * _silu(g_ref[:, sl])).astype(BF16)


def _mla_attn(q, k, v, z, B, S, t, hb):
    T = q.shape[0]
    nq = S // t
    qmap = lambda b, h, i: (b * nq + i, h)
    kvmap = lambda b, h, i: (b, h)
    return pl.pallas_call(
        functools.partial(_mla_attn_body, t=t, hb=hb),
        grid=(B, MLA_HEADS // hb, nq),
        in_specs=[pl.BlockSpec((t, hb * MLA_QK), qmap),
                  pl.BlockSpec((S, hb * MLA_QK), kvmap),
                  pl.BlockSpec((S, hb * MLA_V), kvmap),
                  pl.BlockSpec((t, hb * MLA_V), lambda b, h, i: (b * nq + i, COL_GB // (hb * MLA_V) + h))],
        out_specs=pl.BlockSpec((t, hb * MLA_V), qmap),
        out_shape=jax.ShapeDtypeStruct((T, MLA_WIDTH), BF16),
        scratch_shapes=[pltpu.VMEM((hb, t, 1), F32),
                        pltpu.VMEM((hb, t, 1), F32),
                        pltpu.VMEM((hb, t, MLA_V), F32)],
        compiler_params=_params(("parallel", "parallel", "arbitrary"), 48),
        name="mla_attn",
    )(q, k, v, z)


def _online_update(m_ref, l_ref, acc_ref, s, pv_fn):
    m_prev = m_ref[...]
    m_new = jnp.maximum(m_prev, jnp.max(s, axis=-1, keepdims=True))
    corr = jnp.exp(m_prev - m_new)
    p = jnp.exp(s - m_new)
    l_ref[...] = l_ref[...] * corr + jnp.sum(p, axis=-1, keepdims=True)
    acc_ref[...] = acc_ref[...] * corr + pv_fn(p)
    m_ref[...] = m_new


def _decode_body(pt_ref, qda_ref, kn_ref, vn_ref, qlat_ref, qpe_ref, ckvn_ref, kpen_ref, lv_ref,
                 ck_hbm, kp_hbm, dk_hbm, dv_hbm, oda_ref, olat_ref,
                 ck_buf, kp_buf, dk_buf, dv_buf, sem,
                 m_ml, l_ml, acc_ml, m_da, l_da, acc_da, *, P, NC, page):
    n = pl.program_id(0)
    c = n % NC
    slot = n % 2
    nt = (((1,), (1,)), ((), ()))

    def copies(n_, slot_, real):
        b_ = n_ // NC
        c_ = n_ % NC
        out = []
        for j in range(P):
            pg = pt_ref[b_, c_ * P + j] if real else 0
            out.append(pltpu.make_async_copy(
                ck_hbm.at[pg], ck_buf.at[slot_, pl.ds(j * page, page), :], sem.at[0, slot_]))
            out.append(pltpu.make_async_copy(
                kp_hbm.at[pg], kp_buf.at[slot_, :, pl.ds(j * page, page)], sem.at[1, slot_]))
            out.append(pltpu.make_async_copy(
                dk_hbm.at[pg], dk_buf.at[slot_, pl.ds(j * 2 * page, 2 * page), :], sem.at[2, slot_]))
            out.append(pltpu.make_async_copy(
                dv_hbm.at[pg], dv_buf.at[slot_, pl.ds(j * 2 * page, 2 * page), :], sem.at[3, slot_]))
        return out

    @pl.when(n == 0)
    def _():
        for cp in copies(0, 0, True):
            cp.start()

    @pl.when(n + 1 < pl.num_programs(0))
    def _():
        for cp in copies(n + 1, 1 - slot, True):
            cp.start()

    for cp in copies(n, slot, False):
        cp.wait()

    @pl.when(c == 0)
    def _():
        for m_ref, l_ref, acc_ref in ((m_ml, l_ml, acc_ml), (m_da, l_da, acc_da)):
            m_ref[...] = jnp.full(m_ref.shape, NEG_INF, F32)
            l_ref[...] = jnp.zeros(l_ref.shape, F32)
            acc_ref[...] = jnp.zeros(acc_ref.shape, F32)

    ck = ck_buf[slot].astype(BF16)
    kp = kp_buf[slot].astype(BF16)
    ql = qlat_ref[0]
    qp = qpe_ref[0][:, :MLA_ROPE]
    s = (lax.dot_general(ql, ck, nt, preferred_element_type=F32)
         + jnp.dot(qp, kp, preferred_element_type=F32))
    _online_update(m_ml, l_ml, acc_ml, s,
                   lambda p: jnp.dot(p.astype(BF16), ck, preferred_element_type=F32))

    low = lax.broadcasted_iota(jnp.int32, (1, LANES), 1) < DA_HEAD_DIM
    qda = qda_ref[0].astype(F32)
    parts = []
    for g in range(DA_KV_HEADS):
        q = qda[g * DA_GROUP:(g + 1) * DA_GROUP, :]
        zero = jnp.zeros_like(q)
        parts += [jnp.where(low, q, zero), jnp.where(low, zero, q)]
    q32 = jnp.concatenate(parts, axis=0)
    dk = dk_buf[slot].astype(BF16)
    dv = dv_buf[slot].astype(BF16)
    s = lax.dot_general(q32.astype(BF16), dk, nt, preferred_element_type=F32)
    col_head = lax.broadcasted_iota(jnp.int32, s.shape, 1) % DA_KV_HEADS
    row_head = lax.broadcasted_iota(jnp.int32, s.shape, 0) // (2 * DA_GROUP)
    s = jnp.where(col_head == row_head, s, NEG_INF)
    _online_update(m_da, l_da, acc_da, s,
                   lambda p: jnp.dot(p.astype(BF16), dv, preferred_element_type=F32))

    @pl.when(c == NC - 1)
    def _():
        ckvn = ckvn_ref[0]
        s = (jnp.sum(ql.astype(F32) * ckvn, axis=-1, keepdims=True)
             + jnp.sum(qp.astype(F32) * kpen_ref[0], axis=-1, keepdims=True))
        _online_update(m_ml, l_ml, acc_ml, s, lambda p: p * ckvn)
        olat_ref[0] = acc_ml[...] / l_ml[...]

        per_head = lambda x: jnp.concatenate(
            [jnp.broadcast_to(x[:, g * LANES:(g + 1) * LANES], (2 * DA_GROUP, LANES))
             for g in range(DA_KV_HEADS)], axis=0)
        kn = per_head(kn_ref[0])
        vn = per_head(vn_ref[0])
        s = jnp.sum(q32 * kn, axis=-1, keepdims=True)
        _online_update(m_da, l_da, acc_da, s, lambda p: p * vn)
        o = acc_da[...] / l_da[...]
        lam = _da_lambda(lv_ref[...])
        for g in range(DA_KV_HEADS):
            r0 = g * 2 * DA_GROUP
            oda_ref[0, g * DA_GROUP:(g + 1) * DA_GROUP, :] = (
                o[r0:r0 + DA_GROUP] - lam * o[r0 + DA_GROUP:r0 + 2 * DA_GROUP])


def _decode(page_table, qda, kn, vn, qlat, qpe, ckvn, kpen, lv, ck, kp, dk, dv, P):
    Bd, n_pages = page_table.shape
    page = ck.shape[1]
    NC = n_pages // P
    rows_da = 2 * DA_KV_HEADS * DA_GROUP
    blk = lambda n, pt: (n // NC, 0, 0)
    any_spec = pl.BlockSpec(memory_space=pl.ANY)
    grid_spec = pltpu.PrefetchScalarGridSpec(
        num_scalar_prefetch=1,
        grid=(Bd * NC,),
        in_specs=[pl.BlockSpec((1, DA_HEADS, LANES), blk),
                  pl.BlockSpec((1, 1, DA_KV_HEADS * LANES), blk),
                  pl.BlockSpec((1, 1, DA_KV_HEADS * LANES), blk),
                  pl.BlockSpec((1, MLA_HEADS, MLA_KV_RANK), blk),
                  pl.BlockSpec((1, MLA_HEADS, LANES), blk),
                  pl.BlockSpec((1, 1, MLA_KV_RANK), blk),
                  pl.BlockSpec((1, 1, MLA_ROPE), blk),
                  pl.BlockSpec((4, DA_HEAD_DIM), lambda n, pt: (0, 0)),
                  any_spec, any_spec, any_spec, any_spec],
        out_specs=[pl.BlockSpec((1, DA_HEADS, LANES), blk),
                   pl.BlockSpec((1, MLA_HEADS, MLA_KV_RANK), blk)],
        scratch_shapes=[pltpu.VMEM((2, P * page, MLA_KV_RANK), F32),
                        pltpu.VMEM((2, MLA_ROPE, P * page), F32),
                        pltpu.VMEM((2, P * page * DA_KV_HEADS, LANES), F32),
                        pltpu.VMEM((2, P * page * DA_KV_HEADS, LANES), F32),
                        pltpu.SemaphoreType.DMA((4, 2)),
                        pltpu.VMEM((MLA_HEADS, 1), F32),
                        pltpu.VMEM((MLA_HEADS, 1), F32),
                        pltpu.VMEM((MLA_HEADS, MLA_KV_RANK), F32),
                        pltpu.VMEM((rows_da, 1), F32),
                        pltpu.VMEM((rows_da, 1), F32),
                        pltpu.VMEM((rows_da, LANES), F32)])
    return pl.pallas_call(
        functools.partial(_decode_body, P=P, NC=NC, page=page),
        grid_spec=grid_spec,
        out_shape=[jax.ShapeDtypeStruct((Bd, DA_HEADS, LANES), F32),
                   jax.ShapeDtypeStruct((Bd, MLA_HEADS, MLA_KV_RANK), F32)],
        compiler_params=_params(("arbitrary",), 56),
        name="decode",
    )(page_table, qda, kn, vn, qlat, qpe, ckvn, kpen, lv, ck, kp, dk, dv)


def _sample_post_body(oda_ref, olat_ref, ga_ref, gb_ref, gs_ref, wuv_ref, oa_ref, ob_ref):
    gs = gs_ref[...]
    for h in range(DA_HEADS):
        sl = slice(h * LANES, (h + 1) * LANES)
        od = oda_ref[:, sl]
        y = od * lax.rsqrt(jnp.mean(od * od, axis=-1, keepdims=True) + EPS) * gs * (1.0 - LAM_INIT)
        oa_ref[:, sl] = (y * _silu(ga_ref[:, sl])).astype(BF16)
    for h in range(MLA_HEADS):
        sl = slice(h * MLA_V, (h + 1) * MLA_V)
        ol = olat_ref[:, h * MLA_KV_RANK:(h + 1) * MLA_KV_RANK].astype(BF16)
        ob = jnp.dot(ol, wuv_ref[h], preferred_element_type=F32)
        ob_ref[:, sl] = (ob * _silu(gb_ref[:, sl])).astype(BF16)


def _sample_post(oda, olat, z, gs, wuv):
    Bd = oda.shape[0]
    const = lambda i: (0, 0)
    return pl.pallas_call(
        _sample_post_body,
        grid=(1,),
        in_specs=[pl.BlockSpec((Bd, DA_WIDTH), const),
                  pl.BlockSpec((Bd, MLA_HEADS * MLA_KV_RANK), const),
                  pl.BlockSpec((Bd, DA_WIDTH), lambda i: (0, COL_GA // DA_WIDTH)),
                  pl.BlockSpec((Bd, MLA_WIDTH), lambda i: (0, COL_GB // MLA_WIDTH)),
                  pl.BlockSpec((1, LANES), const),
                  pl.BlockSpec(wuv.shape, lambda i: (0, 0, 0))],
        out_specs=[pl.BlockSpec((Bd, DA_WIDTH), const),
                   pl.BlockSpec((Bd, MLA_WIDTH), const)],
        out_shape=[jax.ShapeDtypeStruct((Bd, DA_WIDTH), BF16),
                   jax.ShapeDtypeStruct((Bd, MLA_WIDTH), BF16)],
        compiler_params=_params(("arbitrary",), 40),
        name="sample_post",
    )(oda, olat, z, z, gs, wuv)


def _outproj_body(oa_ref, ob_ref, w_ref, x_ref, y_ref):
    ka = oa_ref.shape[1]
    y_ref[...] = (x_ref[...]
                  + jnp.dot(oa_ref[...], w_ref[:ka, :], preferred_element_type=F32)
                  + jnp.dot(ob_ref[...], w_ref[ka:, :], preferred_element_type=F32))


def _outproj(oa, ob, w, x, tm, tn):
    T, D = x.shape
    return pl.pallas_call(
        _outproj_body,
        grid=(T // tm, D // tn),
        in_specs=[pl.BlockSpec((tm, oa.shape[1]), lambda i, j: (i, 0)),
                  pl.BlockSpec((tm, ob.shape[1]), lambda i, j: (i, 0)),
                  pl.BlockSpec((w.shape[0], tn), lambda i, j: (0, j)),
                  pl.BlockSpec((tm, tn), lambda i, j: (i, j))],
        out_specs=pl.BlockSpec((tm, tn), lambda i, j: (i, j)),
        out_shape=jax.ShapeDtypeStruct((T, D), F32),
        compiler_params=_params(("parallel", "arbitrary"), 48),
        name="outproj",
    )(oa, ob, w, x)


def _row_tile(T, cap):
    t = min(T, cap)
    while T % t:
        t //= 2
    return t


def kernel(x_prompt, x_sample, cache_diff_k, cache_diff_v, cache_mla_ckv, cache_mla_kpe, page_table,
           g_pre, w_in, da_q_norm, da_k_norm, da_lambda_q1, da_lambda_k1, da_lambda_q2, da_lambda_k2,
           da_subln, mla_q_a_norm, w_uq, mla_qn_norm, mla_qpe_norm, mla_kv_a_norm, mla_kpe_norm,
           w_uk, w_uv, w_out):
    B, S, D = x_prompt.shape
    Bd, Td, _ = x_sample.shape
    depth, n_phys, page = cache_mla_ckv.shape[:3]
    n_pages = page_table.shape[1]
    assert depth == 1 and Td == 1
    assert w_uq.shape[1:] == (MLA_Q_RANK, MLA_HEADS, MLA_NOPE + MLA_ROPE)
    assert w_uk.shape[1:] == (MLA_KV_RANK, MLA_HEADS, MLA_NOPE)
    assert w_uv.shape[1:] == (MLA_KV_RANK, MLA_HEADS, MLA_V)
    assert cache_diff_k.shape[3:] == (DA_KV_HEADS, 2 * DA_HEAD_DIM)
    assert w_in.shape[2] == COL_KPE + MLA_ROPE

    w = w_in[0]
    sizes = (DA_HEADS * 2 * DA_HEAD_DIM, DA_KV_HEADS * 2 * DA_HEAD_DIM, DA_KV_HEADS * DA_V_DIM, DA_WIDTH,
             MLA_Q_RANK, MLA_KV_RANK, MLA_ROPE, MLA_WIDTH)
    offs = [0]
    for s_ in sizes:
        offs.append(offs[-1] + s_)
    sec = [w[:, offs[i]:offs[i + 1]] for i in range(len(sizes))]
    zq, zk, zv, ga, zqa, zckv, zkpe, gb = sec
    w_perm = jnp.concatenate(
        [zq, ga, gb, zqa, zckv, zk, zv, zkpe, jnp.zeros((D, N_PAD - COL_KPE - MLA_ROPE), w.dtype)],
        axis=1).astype(BF16)
    wuq = w_uq[0]
    wuq_pad = jnp.concatenate(
        [wuq, jnp.zeros((MLA_Q_RANK, MLA_HEADS, MLA_QK - MLA_NOPE - MLA_ROPE), wuq.dtype)], axis=-1)
    wuq_pad = wuq_pad.reshape(MLA_Q_RANK, MLA_HEADS * MLA_QK).astype(BF16)
    wkv = jnp.concatenate([w_uk[0].reshape(MLA_KV_RANK, -1), w_uv[0].reshape(MLA_KV_RANK, -1)],
                          axis=1).astype(BF16)
    wuk_t = jnp.transpose(w_uk[0], (1, 2, 0)).astype(BF16)
    wuv_h = jnp.transpose(w_uv[0], (1, 0, 2)).astype(BF16)
    w_o = w_out[0].astype(BF16)

    tile2 = lambda g: jnp.concatenate([g, g], axis=-1)
    pad2 = lambda g: jnp.concatenate([g, jnp.zeros_like(g)], axis=-1)
    gq, gk = tile2(da_q_norm), tile2(da_k_norm)
    gqpe, gkpe = pad2(mla_qpe_norm), pad2(mla_kpe_norm)
    lv = jnp.concatenate([da_lambda_q1, da_lambda_k1, da_lambda_q2, da_lambda_k2], axis=0)

    Tp = B * S
    xp = x_prompt.reshape(Tp, D)
    tm = _row_tile(S, 512)
    pos_p = jnp.arange(S)
    z = _inproj(xp, g_pre, w_perm, tm)
    q_da, k_da, v_da, kb_da, vb_da = _da_prep(z, gq, gk, _rope_tables(pos_p, True), tm, S // tm)
    tm_ml = _row_tile(S, 256)
    ckv_p, kpe_p, q_ml, k_ml, v_ml = _mla_prep(
        z, True, mla_q_a_norm, wuq_pad, mla_qn_norm, gqpe, mla_kv_a_norm, gkpe,
        _rope_tables(pos_p, False), wkv, tm_ml, S // tm_ml)
    o_a = _da_attn(q_da, kb_da, vb_da, z, lv, da_subln, B, S, _row_tile(S, 256))
    o_b = _mla_attn(q_ml, k_ml, v_ml, z, B, S, _row_tile(S, 512), 2)
    y_p = _outproj(o_a, o_b, w_o, xp, _row_tile(Tp, 1024), 512).reshape(B, S, D)

    xs = x_sample.reshape(Bd, D)
    pos_s = jnp.full((Bd,), n_pages * page, jnp.int32)
    zs = _inproj(xs, g_pre, w_perm, Bd)
    qs_da, ks_da, vs_da, _, _ = _da_prep(zs, gq, gk, _rope_tables(pos_s, True), Bd, 1)
    ckv_s, kpe_s, qlat, qpe = _mla_prep(
        zs, False, mla_q_a_norm, wuq_pad, mla_qn_norm, gqpe, mla_kv_a_norm, gkpe,
        _rope_tables(pos_s, False), wuk_t, Bd, 1)
    kvw = DA_KV_HEADS * LANES
    oda, olat = _decode(
        page_table,
        qs_da.reshape(Bd, DA_HEADS, LANES), ks_da.reshape(Bd, 1, kvw), vs_da.reshape(Bd, 1, kvw),
        qlat.reshape(Bd, MLA_HEADS, MLA_KV_RANK), qpe.reshape(Bd, MLA_HEADS, LANES),
        ckv_s.reshape(Bd, 1, MLA_KV_RANK), kpe_s.reshape(Bd, 1, MLA_ROPE), lv,
        cache_mla_ckv.reshape(n_phys, page, MLA_KV_RANK),
        jnp.swapaxes(cache_mla_kpe.reshape(n_phys, page, MLA_ROPE), 1, 2),
        cache_diff_k.reshape(n_phys, page * DA_KV_HEADS, LANES),
        cache_diff_v.reshape(n_phys, page * DA_KV_HEADS, LANES),
        _row_tile(n_pages, 16))
    oa_s, ob_s = _sample_post(oda.reshape(Bd, DA_WIDTH), olat.reshape(Bd, MLA_HEADS * MLA_KV_RANK),
                              zs, da_subln, wuv_h)
    y_s = _outproj(oa_s, ob_s, w_o, xs, Bd, 512).reshape(Bd, 1, D)

    return (y_p, y_s,
            k_da.reshape(1, B, S, DA_KV_HEADS, 2 * DA_HEAD_DIM), v_da.reshape(1, B, S, DA_KV_HEADS, DA_V_DIM),
            ckv_p.reshape(1, B, S, MLA_KV_RANK), kpe_p.reshape(1, B, S, MLA_ROPE),
            ks_da.reshape(1, Bd, 1, DA_KV_HEADS, 2 * DA_HEAD_DIM), vs_da.reshape(1, Bd, 1, DA_KV_HEADS, DA_V_DIM),
            ckv_s.reshape(1, Bd, 1, MLA_KV_RANK), kpe_s.reshape(1, Bd, 1, MLA_ROPE))
```

```python
import functools
import math

import jax
import jax.numpy as jnp
from jax import lax
from jax.experimental import pallas as pl
from jax.experimental.pallas import tpu as pltpu

F32 = jnp.float32
BF16 = jnp.bfloat16

EPS = 1e-6
ROPE_THETA = 10000.0
NEG_INF = -1e30
LANES = 128
MIB = 1024 * 1024

DA_HEAD_DIM = 64
DA_V_DIM = 128
DA_HEADS = 16
DA_KV_HEADS = 2
DA_GROUP = DA_HEADS // DA_KV_HEADS
DA_WIDTH = DA_HEADS * DA_V_DIM
DA_SCALE = DA_HEAD_DIM ** -0.5
MLA_HEADS = 16
MLA_NOPE = 128
MLA_ROPE = 64
MLA_V = 128
MLA_KV_RANK = 512
MLA_Q_RANK = 1024
MLA_WIDTH = MLA_HEADS * MLA_V
MLA_SCALE = (MLA_NOPE + MLA_ROPE) ** -0.5
MLA_QK = 2 * LANES
LAM_INIT = 0.8 - 0.6 * math.exp(-0.3 * 0)

COL_Q = 0
COL_GA = COL_Q + DA_HEADS * 2 * DA_HEAD_DIM
COL_GB = COL_GA + DA_WIDTH
COL_QA = COL_GB + MLA_WIDTH
COL_CKV = COL_QA + MLA_Q_RANK
COL_KV = COL_CKV + MLA_KV_RANK
COL_KPE = COL_KV + 2 * DA_KV_HEADS * DA_V_DIM
INPROJ_TN = 768
N_PAD = COL_KPE + 256
assert N_PAD % INPROJ_TN == 0


def _params(sem, vmem_mib):
    return pltpu.CompilerParams(dimension_semantics=sem, vmem_limit_bytes=vmem_mib * MIB)


def _silu(g):
    return g / (1.0 + jnp.exp(-g))


def _inproj_body(x_ref, g_ref, w_ref, z_ref, h_ref):
    @pl.when(pl.program_id(1) == 0)
    def _():
        x = x_ref[...]
        ms = jnp.mean(x * x, axis=-1, keepdims=True)
        h_ref[...] = (x * lax.rsqrt(ms + EPS) * g_ref[...]).astype(BF16)

    z_ref[...] = jnp.dot(h_ref[...], w_ref[...], preferred_element_type=F32)


def _inproj(x, g, w, tm):
    T, D = x.shape
    N = w.shape[1]
    tn = INPROJ_TN
    return pl.pallas_call(
        _inproj_body,
        grid=(T // tm, N // tn),
        in_specs=[pl.BlockSpec((tm, D), lambda i, j: (i, 0)),
                  pl.BlockSpec((1, D), lambda i, j: (0, 0)),
                  pl.BlockSpec((D, tn), lambda i, j: (0, j))],
        out_specs=pl.BlockSpec((tm, tn), lambda i, j: (i, j)),
        out_shape=jax.ShapeDtypeStruct((T, N), F32),
        scratch_shapes=[pltpu.VMEM((tm, D), BF16)],
        compiler_params=_params(("parallel", "arbitrary"), 48),
        name="inproj",
    )(x, g, w)


def _chunk_mean_sq(x, m):
    s = x * x
    hi = s.astype(BF16)
    lo = (s - hi.astype(F32)).astype(BF16)
    return (jnp.dot(hi, m, preferred_element_type=F32)
            + jnp.dot(lo, m, preferred_element_type=F32))


def _norm_rope(x, m, gain, cos, sa, sb):
    y = x * lax.rsqrt(_chunk_mean_sq(x, m) + EPS) * gain
    return y * cos + pltpu.roll(y, 96, 1) * sa + pltpu.roll(y, 32, 1) * sb


def _rope_tables(pos, dual):
    d = DA_HEAD_DIM
    inv = jnp.power(ROPE_THETA, -jnp.arange(0, d, 2, dtype=F32) / d)
    ang = pos.astype(F32)[:, None] * inv[None, :]
    ang = jnp.concatenate([ang, ang], axis=-1)
    cos, sin = jnp.cos(ang), jnp.sin(ang)
    low = jnp.arange(d) < d // 2
    sa = jnp.where(low, -sin, 0.0)
    sb = jnp.where(low, 0.0, sin)
    if dual:
        return tuple(jnp.concatenate([t, t], axis=-1) for t in (cos, sa, sb))
    return tuple(jnp.concatenate([t, jnp.zeros_like(t)], axis=-1) for t in (cos, sa, sb))


def _avg_matrix(chunk, rows):
    i = jnp.arange(LANES)[:, None]
    j = jnp.arange(LANES)[None, :]
    same = (i // chunk == j // chunk) if rows == LANES else (i < rows) & (j >= 0)
    return jnp.where(same, 1.0 / chunk, 0.0).astype(BF16)


def _da_prep_body(zq_ref, zkv_ref, m_ref, gq_ref, gk_ref, cos_ref, sa_ref, sb_ref,
                  q_ref, k_ref, v_ref, kb_ref, vb_ref):
    m = m_ref[...]
    cos, sa, sb = cos_ref[...], sa_ref[...], sb_ref[...]
    gq, gk = gq_ref[...], gk_ref[...]
    for c in range(DA_HEADS):
        sl = slice(c * LANES, (c + 1) * LANES)
        q_ref[:, sl] = (_norm_rope(zq_ref[:, sl], m, gq, cos, sa, sb) * DA_SCALE).astype(BF16)
    for c in range(DA_KV_HEADS):
        sl = slice(c * LANES, (c + 1) * LANES)
        k = _norm_rope(zkv_ref[:, sl], m, gk, cos, sa, sb)
        k_ref[:, sl] = k
        kb_ref[:, sl] = k.astype(BF16)
    v = zkv_ref[:, DA_KV_HEADS * LANES:]
    v_ref[...] = v
    vb_ref[...] = v.astype(BF16)


def _da_prep(z, gq, gk, tables, tm, n_pos_blocks):
    T = z.shape[0]
    qw = DA_HEADS * LANES
    kw = DA_KV_HEADS * LANES
    row = lambda i: (i, 0)
    const = lambda i: (0, 0)
    tab = lambda i: (i % n_pos_blocks, 0)
    return pl.pallas_call(
        _da_prep_body,
        grid=(T // tm,),
        in_specs=[pl.BlockSpec((tm, qw), lambda i: (i, COL_Q // qw)),
                  pl.BlockSpec((tm, 2 * kw), lambda i: (i, COL_KV // (2 * kw))),
                  pl.BlockSpec((LANES, LANES), const),
                  pl.BlockSpec((1, LANES), const),
                  pl.BlockSpec((1, LANES), const),
                  pl.BlockSpec((tm, LANES), tab),
                  pl.BlockSpec((tm, LANES), tab),
                  pl.BlockSpec((tm, LANES), tab)],
        out_specs=[pl.BlockSpec((tm, qw), row),
                   pl.BlockSpec((tm, kw), row),
                   pl.BlockSpec((tm, kw), row),
                   pl.BlockSpec((tm, kw), row),
                   pl.BlockSpec((tm, kw), row)],
        out_shape=[jax.ShapeDtypeStruct((T, qw), BF16),
                   jax.ShapeDtypeStruct((T, kw), F32),
                   jax.ShapeDtypeStruct((T, kw), F32),
                   jax.ShapeDtypeStruct((T, kw), BF16),
                   jax.ShapeDtypeStruct((T, kw), BF16)],
        compiler_params=_params(("parallel",), 40),
        name="da_prep",
    )(z, z, _avg_matrix(DA_HEAD_DIM, LANES), gq, gk, *tables)


def _mla_prep_body(prompt, zqa_ref, zckv_ref, zkpe_ref, gqa_ref, wuq_ref, m128_ref, m64_ref,
                   gqn_ref, gqpe_ref, gckv_ref, gkpe_ref, cos_ref, sa_ref, sb_ref, wx_ref,
                   ckv_ref, kpe_ref, *outs):
    m128, m64 = m128_ref[...], m64_ref[...]
    cos, sa, sb = cos_ref[...], sa_ref[...], sb_ref[...]
    gqn, gqpe = gqn_ref[...], gqpe_ref[...]

    xa = zqa_ref[...]
    qa = (xa * lax.rsqrt(jnp.mean(xa * xa, axis=-1, keepdims=True) + EPS) * gqa_ref[...]).astype(BF16)
    xc = zckv_ref[...]
    ckv = xc * lax.rsqrt(jnp.mean(xc * xc, axis=-1, keepdims=True) + EPS) * gckv_ref[...]
    kpe = _norm_rope(zkpe_ref[...], m64, gkpe_ref[...], cos, sa, sb)
    ckv_ref[...] = ckv
    kpe_ref[...] = kpe[:, :MLA_ROPE]

    for h in range(MLA_HEADS):
        qm = jnp.dot(qa, wuq_ref[:, h * MLA_QK:(h + 1) * MLA_QK], preferred_element_type=F32)
        qn = qm[:, :LANES]
        qn = qn * lax.rsqrt(_chunk_mean_sq(qn, m128) + EPS) * gqn
        qp = _norm_rope(qm[:, LANES:], m64, gqpe, cos, sa, sb)
        if prompt:
            q_ref = outs[0]
            q_ref[:, h * MLA_QK:h * MLA_QK + LANES] = (qn * MLA_SCALE).astype(BF16)
            q_ref[:, h * MLA_QK + LANES:(h + 1) * MLA_QK] = (qp * MLA_SCALE).astype(BF16)
        else:
            qlat_ref, qpe_ref = outs
            qlat = jnp.dot(qn.astype(BF16), wx_ref[h], preferred_element_type=F32)
            qlat_ref[:, h * MLA_KV_RANK:(h + 1) * MLA_KV_RANK] = (qlat * MLA_SCALE).astype(BF16)
            qpe_ref[:, h * LANES:(h + 1) * LANES] = (qp * MLA_SCALE).astype(BF16)

    if prompt:
        _, k_ref, v_ref = outs
        kv = jnp.dot(ckv.astype(BF16), wx_ref[...], preferred_element_type=F32)
        kpb = kpe.astype(BF16)
        for h in range(MLA_HEADS):
            k_ref[:, h * MLA_QK:h * MLA_QK + LANES] = kv[:, h * LANES:(h + 1) * LANES].astype(BF16)
            k_ref[:, h * MLA_QK + LANES:(h + 1) * MLA_QK] = kpb
        v_ref[...] = kv[:, MLA_HEADS * MLA_NOPE:].astype(BF16)


def _mla_prep(z, prompt, gqa, wuq, gqn, gqpe, gckv, gkpe, tables, wx, tm, n_pos_blocks):
    T = z.shape[0]
    row = lambda i: (i, 0)
    const = lambda i: (0, 0)
    tab = lambda i: (i % n_pos_blocks, 0)
    if prompt:
        wx_spec = pl.BlockSpec(wx.shape, const)
        extra_specs = [pl.BlockSpec((tm, MLA_HEADS * MLA_QK), row),
                       pl.BlockSpec((tm, MLA_HEADS * MLA_QK), row),
                       pl.BlockSpec((tm, MLA_WIDTH), row)]
        extra_shapes = [jax.ShapeDtypeStruct((T, MLA_HEADS * MLA_QK), BF16),
                        jax.ShapeDtypeStruct((T, MLA_HEADS * MLA_QK), BF16),
                        jax.ShapeDtypeStruct((T, MLA_WIDTH), BF16)]
    else:
        wx_spec = pl.BlockSpec(wx.shape, lambda i: (0, 0, 0))
        extra_specs = [pl.BlockSpec((tm, MLA_HEADS * MLA_KV_RANK), row),
                       pl.BlockSpec((tm, MLA_HEADS * LANES), row)]
        extra_shapes = [jax.ShapeDtypeStruct((T, MLA_HEADS * MLA_KV_RANK), BF16),
                        jax.ShapeDtypeStruct((T, MLA_HEADS * LANES), BF16)]
    return pl.pallas_call(
        functools.partial(_mla_prep_body, prompt),
        grid=(T // tm,),
        in_specs=[pl.BlockSpec((tm, MLA_Q_RANK), lambda i: (i, COL_QA // MLA_Q_RANK)),
                  pl.BlockSpec((tm, MLA_KV_RANK), lambda i: (i, COL_CKV // MLA_KV_RANK)),
                  pl.BlockSpec((tm, LANES), lambda i: (i, COL_KPE // LANES)),
                  pl.BlockSpec((1, MLA_Q_RANK), const),
                  pl.BlockSpec(wuq.shape, const),
                  pl.BlockSpec((LANES, LANES), const),
                  pl.BlockSpec((LANES, LANES), const),
                  pl.BlockSpec((1, LANES), const),
                  pl.BlockSpec((1, LANES), const),
                  pl.BlockSpec((1, MLA_KV_RANK), const),
                  pl.BlockSpec((1, LANES), const),
                  pl.BlockSpec((tm, LANES), tab),
                  pl.BlockSpec((tm, LANES), tab),
                  pl.BlockSpec((tm, LANES), tab),
                  wx_spec],
        out_specs=[pl.BlockSpec((tm, MLA_KV_RANK), row),
                   pl.BlockSpec((tm, MLA_ROPE), row)] + extra_specs,
        out_shape=[jax.ShapeDtypeStruct((T, MLA_KV_RANK), F32),
                   jax.ShapeDtypeStruct((T, MLA_ROPE), F32)] + extra_shapes,
        compiler_params=_params(("parallel",), 56),
        name="mla_prep_prompt" if prompt else "mla_prep_sample",
    )(z, z, z, gqa, wuq, _avg_matrix(LANES, LANES), _avg_matrix(MLA_ROPE, MLA_ROPE),
      gqn, gqpe, gckv, gkpe, *tables, wx)


def _da_lambda(lv):
    e1 = jnp.exp(jnp.sum(lv[0:1] * lv[1:2], axis=-1, keepdims=True))
    e2 = jnp.exp(jnp.sum(lv[2:3] * lv[3:4], axis=-1, keepdims=True))
    return e1 - e2 + LAM_INIT


def _softmax_step_t(s, m_ref, l_ref, acc_ref, idx, v_t):
    m_prev = m_ref[idx]
    m_new = jnp.maximum(m_prev, jnp.max(s, axis=0, keepdims=True))
    corr = jnp.exp(m_prev - m_new)
    p = jnp.exp(s - m_new)
    l_ref[idx] = l_ref[idx] * corr + jnp.sum(p, axis=0, keepdims=True)
    acc_ref[idx] = acc_ref[idx] * corr + jnp.dot(v_t, p.astype(BF16), preferred_element_type=F32)
    m_ref[idx] = m_new


def _da_attn_body(qt_ref, k_ref, vt_ref, g_ref, lv_ref, gs_ref, o_ref, qbd_ref, m_ref, l_ref, acc_ref, *, t):
    qi = pl.program_id(2)
    cols = 2 * DA_GROUP * t
    low = lax.broadcasted_iota(jnp.int32, (LANES, 1), 0) < DA_HEAD_DIM
    for r in range(DA_GROUP):
        q = qt_ref[r * LANES:(r + 1) * LANES, :]
        zero = jnp.zeros_like(q)
        qbd_ref[:, (2 * r) * t:(2 * r + 1) * t] = jnp.where(low, q, zero)
        qbd_ref[:, (2 * r + 1) * t:(2 * r + 2) * t] = jnp.where(low, zero, q)
    m_ref[...] = jnp.full(m_ref.shape, NEG_INF, F32)
    l_ref[...] = jnp.zeros(l_ref.shape, F32)
    acc_ref[...] = jnp.zeros(acc_ref.shape, F32)

    def step(ki, masked):
        k = k_ref[pl.ds(pl.multiple_of(ki * t, t), t), :]
        s = jnp.dot(k, qbd_ref[...], preferred_element_type=F32)
        if masked:
            key = lax.broadcasted_iota(jnp.int32, (t, cols), 0)
            qry = lax.broadcasted_iota(jnp.int32, (t, cols), 1) % t
            s = jnp.where(key <= qry, s, NEG_INF)
        _softmax_step_t(s, m_ref, l_ref, acc_ref, 0, vt_ref[0, ki])

    def loop_body(ki, carry):
        step(ki, False)
        return carry

    lax.fori_loop(0, qi, loop_body, 0)
    step(qi, True)

    lam = _da_lambda(lv_ref[...])
    gs = gs_ref[...] * (1.0 - LAM_INIT)
    inv_l = 1.0 / l_ref[0]
    for r in range(DA_GROUP):
        a1 = slice((2 * r) * t, (2 * r + 1) * t)
        a2 = slice((2 * r + 1) * t, (2 * r + 2) * t)
        od = acc_ref[0, :, a1] * inv_l[:, a1] - lam * (acc_ref[0, :, a2] * inv_l[:, a2])
        y = (od * lax.rsqrt(jnp.mean(od * od, axis=0, keepdims=True) + EPS)).T
        sl = slice(r * LANES, (r + 1) * LANES)
        o_ref[:, sl] = (y * gs * _silu(g_ref[:, sl])).astype(BF16)


def _da_attn(qt, kb, vt, z, lv, gs, B, S, t):
    T = kb.shape[0]
    nq = S // t
    gw = DA_GROUP * LANES
    cols = 2 * DA_GROUP * t
    return pl.pallas_call(
        functools.partial(_da_attn_body, t=t),
        grid=(B, DA_KV_HEADS, nq),
        in_specs=[pl.BlockSpec((gw, t), lambda b, g, i: (g, b * nq + i)),
                  pl.BlockSpec((S, LANES), lambda b, g, i: (b, g)),
                  pl.BlockSpec((1, nq, DA_V_DIM, t), lambda b, g, i: (g, b, 0, 0)),
                  pl.BlockSpec((t, gw), lambda b, g, i: (b * nq + i, COL_GA // gw + g)),
                  pl.BlockSpec((4, DA_HEAD_DIM), lambda b, g, i: (0, 0)),
                  pl.BlockSpec((1, LANES), lambda b, g, i: (0, 0))],
        out_specs=pl.BlockSpec((t, gw), lambda b, g, i: (b * nq + i, g)),
        out_shape=jax.ShapeDtypeStruct((T, DA_WIDTH), BF16),
        scratch_shapes=[pltpu.VMEM((LANES, cols), BF16),
                        pltpu.VMEM((1, 1, cols), F32),
                        pltpu.VMEM((1, 1, cols), F32),
                        pltpu.VMEM((1, DA_V_DIM, cols), F32)],
        compiler_params=_params(("parallel", "parallel", "arbitrary"), 48),
        name="da_attn",
    )(qt, kb, vt, z, lv, gs)


def _mla_attn_body(qt_ref, k_ref, vt_ref, g_ref, o_ref, m_ref, l_ref, acc_ref, *, t, hb):
    qi = pl.program_id(2)
    m_ref[...] = jnp.full(m_ref.shape, NEG_INF, F32)
    l_ref[...] = jnp.zeros(l_ref.shape, F32)
    acc_ref[...] = jnp.zeros(acc_ref.shape, F32)

    def step(ki, masked):
        row0 = pl.multiple_of(ki * t, t)
        for h in range(hb):
            k = k_ref[pl.ds(row0, t), h * MLA_QK:(h + 1) * MLA_QK]
            s = jnp.dot(k, qt_ref[h * MLA_QK:(h + 1) * MLA_QK, :], preferred_element_type=F32)
            if masked:
                key = lax.broadcasted_iota(jnp.int32, (t, t), 0)
                qry = lax.broadcasted_iota(jnp.int32, (t, t), 1)
                s = jnp.where(key <= qry, s, NEG_INF)
            _softmax_step_t(s, m_ref, l_ref, acc_ref, h, vt_ref[h, ki])

    def loop_body(ki, carry):
        step(ki, False)
        return carry

    lax.fori_loop(0, qi, loop_body, 0)
    step(qi, True)

    for h in range(hb):
        sl = slice(h * MLA_V, (h + 1) * MLA_V)
        o = (acc_ref[h] * (1.0 / l_ref[h])).T
        o_ref[:, sl] = (o * _silu(g_ref[:, sl])).astype(BF16)


def _mla_attn(qt, k, vt, z, B, S, t, hb):
    T = k.shape[0]
    nq = S // t
    return pl.pallas_call(
        functools.partial(_mla_attn_body, t=t, hb=hb),
        grid=(B, MLA_HEADS // hb, nq),
        in_specs=[pl.BlockSpec((hb * MLA_QK, t), lambda b, h, i: (h, b * nq + i)),
                  pl.BlockSpec((S, hb * MLA_QK), lambda b, h, i: (b, h)),
                  pl.BlockSpec((hb, nq, MLA_V, t), lambda b, h, i: (h, b, 0, 0)),
                  pl.BlockSpec((t, hb * MLA_V), lambda b, h, i: (b * nq + i, COL_GB // (hb * MLA_V) + h))],
        out_specs=pl.BlockSpec((t, hb * MLA_V), lambda b, h, i: (b * nq + i, h)),
        out_shape=jax.ShapeDtypeStruct((T, MLA_WIDTH), BF16),
        scratch_shapes=[pltpu.VMEM((hb, 1, t), F32),
                        pltpu.VMEM((hb, 1, t), F32),
                        pltpu.VMEM((hb, MLA_V, t), F32)],
        compiler_params=_params(("parallel", "parallel", "arbitrary"), 48),
        name="mla_attn",
    )(qt, k, vt, z)


def _online_update(m_ref, l_ref, acc_ref, s, pv_fn):
    m_prev = m_ref[...]
    m_new = jnp.maximum(m_prev, jnp.max(s, axis=-1, keepdims=True))
    corr = jnp.exp(m_prev - m_new)
    p = jnp.exp(s - m_new)
    l_ref[...] = l_ref[...] * corr + jnp.sum(p, axis=-1, keepdims=True)
    acc_ref[...] = acc_ref[...] * corr + pv_fn(p)
    m_ref[...] = m_new


def _decode_body(pt_ref, qda_ref, kn_ref, vn_ref, qlat_ref, qpe_ref, ckvn_ref, kpen_ref, lv_ref,
                 ck_hbm, kp_hbm, dk_hbm, dv_hbm, oda_ref, olat_ref,
                 ck_buf, kp_buf, dk_buf, dv_buf, sem,
                 m_ml, l_ml, acc_ml, m_da, l_da, acc_da, *, P, NC, page):
    n = pl.program_id(0)
    c = n % NC
    slot = n % 2
    nt = (((1,), (1,)), ((), ()))

    def copies(n_, slot_, real):
        b_ = n_ // NC
        c_ = n_ % NC
        out = []
        for j in range(P):
            pg = pt_ref[b_, c_ * P + j] if real else 0
            out.append(pltpu.make_async_copy(
                ck_hbm.at[pg], ck_buf.at[slot_, pl.ds(j * page, page), :], sem.at[0, slot_]))
            out.append(pltpu.make_async_copy(
                kp_hbm.at[pg], kp_buf.at[slot_, :, pl.ds(j * page, page)], sem.at[1, slot_]))
            out.append(pltpu.make_async_copy(
                dk_hbm.at[pg], dk_buf.at[slot_, pl.ds(j * 2 * page, 2 * page), :], sem.at[2, slot_]))
            out.append(pltpu.make_async_copy(
                dv_hbm.at[pg], dv_buf.at[slot_, pl.ds(j * 2 * page, 2 * page), :], sem.at[3, slot_]))
        return out

    @pl.when(n == 0)
    def _():
        for cp in copies(0, 0, True):
            cp.start()

    @pl.when(n + 1 < pl.num_programs(0))
    def _():
        for cp in copies(n + 1, 1 - slot, True):
            cp.start()

    for cp in copies(n, slot, False):
        cp.wait()

    @pl.when(c == 0)
    def _():
        for m_ref, l_ref, acc_ref in ((m_ml, l_ml, acc_ml), (m_da, l_da, acc_da)):
            m_ref[...] = jnp.full(m_ref.shape, NEG_INF, F32)
            l_ref[...] = jnp.zeros(l_ref.shape, F32)
            acc_ref[...] = jnp.zeros(acc_ref.shape, F32)

    ck = ck_buf[slot].astype(BF16)
    kp = kp_buf[slot].astype(BF16)
    ql = qlat_ref[0]
    qp = qpe_ref[0][:, :MLA_ROPE]
    s = (lax.dot_general(ql, ck, nt, preferred_element_type=F32)
         + jnp.dot(qp, kp, preferred_element_type=F32))
    _online_update(m_ml, l_ml, acc_ml, s,
                   lambda p: jnp.dot(p.astype(BF16), ck, preferred_element_type=F32))

    low = lax.broadcasted_iota(jnp.int32, (1, LANES), 1) < DA_HEAD_DIM
    qda = qda_ref[0].astype(F32)
    parts = []
    for g in range(DA_KV_HEADS):
        q = qda[g * DA_GROUP:(g + 1) * DA_GROUP, :]
        zero = jnp.zeros_like(q)
        parts += [jnp.where(low, q, zero), jnp.where(low, zero, q)]
    q32 = jnp.concatenate(parts, axis=0)
    dk = dk_buf[slot].astype(BF16)
    dv = dv_buf[slot].astype(BF16)
    s = lax.dot_general(q32.astype(BF16), dk, nt, preferred_element_type=F32)
    col_head = lax.broadcasted_iota(jnp.int32, s.shape, 1) % DA_KV_HEADS
    row_head = lax.broadcasted_iota(jnp.int32, s.shape, 0) // (2 * DA_GROUP)
    s = jnp.where(col_head == row_head, s, NEG_INF)
    _online_update(m_da, l_da, acc_da, s,
                   lambda p: jnp.dot(p.astype(BF16), dv, preferred_element_type=F32))

    @pl.when(c == NC - 1)
    def _():
        ckvn = ckvn_ref[0]
        s = (jnp.sum(ql.astype(F32) * ckvn, axis=-1, keepdims=True)
             + jnp.sum(qp.astype(F32) * kpen_ref[0], axis=-1, keepdims=True))
        _online_update(m_ml, l_ml, acc_ml, s, lambda p: p * ckvn)
        olat_ref[0] = acc_ml[...] / l_ml[...]

        per_head = lambda x: jnp.concatenate(
            [jnp.broadcast_to(x[:, g * LANES:(g + 1) * LANES], (2 * DA_GROUP, LANES))
             for g in range(DA_KV_HEADS)], axis=0)
        kn = per_head(kn_ref[0])
        vn = per_head(vn_ref[0])
        s = jnp.sum(q32 * kn, axis=-1, keepdims=True)
        _online_update(m_da, l_da, acc_da, s, lambda p: p * vn)
        o = acc_da[...] / l_da[...]
        lam = _da_lambda(lv_ref[...])
        for g in range(DA_KV_HEADS):
            r0 = g * 2 * DA_GROUP
            oda_ref[0, g * DA_GROUP:(g + 1) * DA_GROUP, :] = (
                o[r0:r0 + DA_GROUP] - lam * o[r0 + DA_GROUP:r0 + 2 * DA_GROUP])


def _decode(page_table, qda, kn, vn, qlat, qpe, ckvn, kpen, lv, ck, kp, dk, dv, P):
    Bd, n_pages = page_table.shape
    page = ck.shape[1]
    NC = n_pages // P
    rows_da = 2 * DA_KV_HEADS * DA_GROUP
    blk = lambda n, pt: (n // NC, 0, 0)
    any_spec = pl.BlockSpec(memory_space=pl.ANY)
    grid_spec = pltpu.PrefetchScalarGridSpec(
        num_scalar_prefetch=1,
        grid=(Bd * NC,),
        in_specs=[pl.BlockSpec((1, DA_HEADS, LANES), blk),
                  pl.BlockSpec((1, 1, DA_KV_HEADS * LANES), blk),
                  pl.BlockSpec((1, 1, DA_KV_HEADS * LANES), blk),
                  pl.BlockSpec((1, MLA_HEADS, MLA_KV_RANK), blk),
                  pl.BlockSpec((1, MLA_HEADS, LANES), blk),
                  pl.BlockSpec((1, 1, MLA_KV_RANK), blk),
                  pl.BlockSpec((1, 1, MLA_ROPE), blk),
                  pl.BlockSpec((4, DA_HEAD_DIM), lambda n, pt: (0, 0)),
                  any_spec, any_spec, any_spec, any_spec],
        out_specs=[pl.BlockSpec((1, DA_HEADS, LANES), blk),
                   pl.BlockSpec((1, MLA_HEADS, MLA_KV_RANK), blk)],
        scratch_shapes=[pltpu.VMEM((2, P * page, MLA_KV_RANK), F32),
                        pltpu.VMEM((2, MLA_ROPE, P * page), F32),
                        pltpu.VMEM((2, P * page * DA_KV_HEADS, LANES), F32),
                        pltpu.VMEM((2, P * page * DA_KV_HEADS, LANES), F32),
                        pltpu.SemaphoreType.DMA((4, 2)),
                        pltpu.VMEM((MLA_HEADS, 1), F32),
                        pltpu.VMEM((MLA_HEADS, 1), F32),
                        pltpu.VMEM((MLA_HEADS, MLA_KV_RANK), F32),
                        pltpu.VMEM((rows_da, 1), F32),
                        pltpu.VMEM((rows_da, 1), F32),
                        pltpu.VMEM((rows_da, LANES), F32)])
    return pl.pallas_call(
        functools.partial(_decode_body, P=P, NC=NC, page=page),
        grid_spec=grid_spec,
        out_shape=[jax.ShapeDtypeStruct((Bd, DA_HEADS, LANES), F32),
                   jax.ShapeDtypeStruct((Bd, MLA_HEADS, MLA_KV_RANK), F32)],
        compiler_params=_params(("arbitrary",), 56),
        name="decode",
    )(page_table, qda, kn, vn, qlat, qpe, ckvn, kpen, lv, ck, kp, dk, dv)


def _sample_post_body(oda_ref, olat_ref, ga_ref, gb_ref, gs_ref, wuv_ref, oa_ref, ob_ref):
    gs = gs_ref[...]
    for h in range(DA_HEADS):
        sl = slice(h * LANES, (h + 1) * LANES)
        od = oda_ref[:, sl]
        y = od * lax.rsqrt(jnp.mean(od * od, axis=-1, keepdims=True) + EPS) * gs * (1.0 - LAM_INIT)
        oa_ref[:, sl] = (y * _silu(ga_ref[:, sl])).astype(BF16)
    for h in range(MLA_HEADS):
        sl = slice(h * MLA_V, (h + 1) * MLA_V)
        ol = olat_ref[:, h * MLA_KV_RANK:(h + 1) * MLA_KV_RANK].astype(BF16)
        ob = jnp.dot(ol, wuv_ref[h], preferred_element_type=F32)
        ob_ref[:, sl] = (ob * _silu(gb_ref[:, sl])).astype(BF16)


def _sample_post(oda, olat, z, gs, wuv):
    Bd = oda.shape[0]
    const = lambda i: (0, 0)
    return pl.pallas_call(
        _sample_post_body,
        grid=(1,),
        in_specs=[pl.BlockSpec((Bd, DA_WIDTH), const),
                  pl.BlockSpec((Bd, MLA_HEADS * MLA_KV_RANK), const),
                  pl.BlockSpec((Bd, DA_WIDTH), lambda i: (0, COL_GA // DA_WIDTH)),
                  pl.BlockSpec((Bd, MLA_WIDTH), lambda i: (0, COL_GB // MLA_WIDTH)),
                  pl.BlockSpec((1, LANES), const),
                  pl.BlockSpec(wuv.shape, lambda i: (0, 0, 0))],
        out_specs=[pl.BlockSpec((Bd, DA_WIDTH), const),
                   pl.BlockSpec((Bd, MLA_WIDTH), const)],
        out_shape=[jax.ShapeDtypeStruct((Bd, DA_WIDTH), BF16),
                   jax.ShapeDtypeStruct((Bd, MLA_WIDTH), BF16)],
        compiler_params=_params(("arbitrary",), 40),
        name="sample_post",
    )(oda, olat, z, z, gs, wuv)


def _outproj_body(oa_ref, ob_ref, w_ref, x_ref, y_ref):
    ka = oa_ref.shape[1]
    y_ref[...] = (x_ref[...]
                  + jnp.dot(oa_ref[...], w_ref[:ka, :], preferred_element_type=F32)
                  + jnp.dot(ob_ref[...], w_ref[ka:, :], preferred_element_type=F32))


def _outproj(oa, ob, w, x, tm, tn):
    T, D = x.shape
    return pl.pallas_call(
        _outproj_body,
        grid=(T // tm, D // tn),
        in_specs=[pl.BlockSpec((tm, oa.shape[1]), lambda i, j: (i, 0)),
                  pl.BlockSpec((tm, ob.shape[1]), lambda i, j: (i, 0)),
                  pl.BlockSpec((w.shape[0], tn), lambda i, j: (0, j)),
                  pl.BlockSpec((tm, tn), lambda i, j: (i, j))],
        out_specs=pl.BlockSpec((tm, tn), lambda i, j: (i, j)),
        out_shape=jax.ShapeDtypeStruct((T, D), F32),
        compiler_params=_params(("parallel", "arbitrary"), 48),
        name="outproj",
    )(oa, ob, w, x)


def _row_tile(T, cap):
    t = min(T, cap)
    while T % t:
        t //= 2
    return t


def kernel(x_prompt, x_sample, cache_diff_k, cache_diff_v, cache_mla_ckv, cache_mla_kpe, page_table,
           g_pre, w_in, da_q_norm, da_k_norm, da_lambda_q1, da_lambda_k1, da_lambda_q2, da_lambda_k2,
           da_subln, mla_q_a_norm, w_uq, mla_qn_norm, mla_qpe_norm, mla_kv_a_norm, mla_kpe_norm,
           w_uk, w_uv, w_out):
    B, S, D = x_prompt.shape
    Bd, Td, _ = x_sample.shape
    depth, n_phys, page = cache_mla_ckv.shape[:3]
    n_pages = page_table.shape[1]
    assert depth == 1 and Td == 1
    assert w_uq.shape[1:] == (MLA_Q_RANK, MLA_HEADS, MLA_NOPE + MLA_ROPE)
    assert w_uk.shape[1:] == (MLA_KV_RANK, MLA_HEADS, MLA_NOPE)
    assert w_uv.shape[1:] == (MLA_KV_RANK, MLA_HEADS, MLA_V)
    assert cache_diff_k.shape[3:] == (DA_KV_HEADS, 2 * DA_HEAD_DIM)
    assert w_in.shape[2] == COL_KPE + MLA_ROPE

    w = w_in[0]
    sizes = (DA_HEADS * 2 * DA_HEAD_DIM, DA_KV_HEADS * 2 * DA_HEAD_DIM, DA_KV_HEADS * DA_V_DIM, DA_WIDTH,
             MLA_Q_RANK, MLA_KV_RANK, MLA_ROPE, MLA_WIDTH)
    offs = [0]
    for s_ in sizes:
        offs.append(offs[-1] + s_)
    sec = [w[:, offs[i]:offs[i + 1]] for i in range(len(sizes))]
    zq, zk, zv, ga, zqa, zckv, zkpe, gb = sec
    w_perm = jnp.concatenate(
        [zq, ga, gb, zqa, zckv, zk, zv, zkpe, jnp.zeros((D, N_PAD - COL_KPE - MLA_ROPE), w.dtype)],
        axis=1).astype(BF16)
    wuq = w_uq[0]
    wuq_pad = jnp.concatenate(
        [wuq, jnp.zeros((MLA_Q_RANK, MLA_HEADS, MLA_QK - MLA_NOPE - MLA_ROPE), wuq.dtype)], axis=-1)
    wuq_pad = wuq_pad.reshape(MLA_Q_RANK, MLA_HEADS * MLA_QK).astype(BF16)
    wkv = jnp.concatenate([w_uk[0].reshape(MLA_KV_RANK, -1), w_uv[0].reshape(MLA_KV_RANK, -1)],
                          axis=1).astype(BF16)
    wuk_t = jnp.transpose(w_uk[0], (1, 2, 0)).astype(BF16)
    wuv_h = jnp.transpose(w_uv[0], (1, 0, 2)).astype(BF16)
    w_o = w_out[0].astype(BF16)

    tile2 = lambda g: jnp.concatenate([g, g], axis=-1)
    pad2 = lambda g: jnp.concatenate([g, jnp.zeros_like(g)], axis=-1)
    gq, gk = tile2(da_q_norm), tile2(da_k_norm)
    gqpe, gkpe = pad2(mla_qpe_norm), pad2(mla_kpe_norm)
    lv = jnp.concatenate([da_lambda_q1, da_lambda_k1, da_lambda_q2, da_lambda_k2], axis=0)

    Tp = B * S
    xp = x_prompt.reshape(Tp, D)
    tm = _row_tile(S, 512)
    pos_p = jnp.arange(S)
    z = _inproj(xp, g_pre, w_perm, tm)
    q_da, k_da, v_da, kb_da, vb_da = _da_prep(z, gq, gk, _rope_tables(pos_p, True), tm, S // tm)
    tm_ml = _row_tile(S, 256)
    ckv_p, kpe_p, q_ml, k_ml, v_ml = _mla_prep(
        z, True, mla_q_a_norm, wuq_pad, mla_qn_norm, gqpe, mla_kv_a_norm, gkpe,
        _rope_tables(pos_p, False), wkv, tm_ml, S // tm_ml)
    t_da = _row_tile(S, 256)
    t_ml = _row_tile(S, 512)
    key_tiles_t = lambda v, t, heads, dim: jnp.transpose(
        v.reshape(Tp // t, t, heads, dim), (2, 0, 3, 1))
    o_a = _da_attn(q_da.T, kb_da, key_tiles_t(vb_da, t_da, DA_KV_HEADS, DA_V_DIM), z, lv, da_subln,
                   B, S, t_da)
    o_b = _mla_attn(q_ml.T, k_ml, key_tiles_t(v_ml, t_ml, MLA_HEADS, MLA_V), z, B, S, t_ml, 2)
    y_p = _outproj(o_a, o_b, w_o, xp, _row_tile(Tp, 1024), 512).reshape(B, S, D)

    xs = x_sample.reshape(Bd, D)
    pos_s = jnp.full((Bd,), n_pages * page, jnp.int32)
    zs = _inproj(xs, g_pre, w_perm, Bd)
    qs_da, ks_da, vs_da, _, _ = _da_prep(zs, gq, gk, _rope_tables(pos_s, True), Bd, 1)
    ckv_s, kpe_s, qlat, qpe = _mla_prep(
        zs, False, mla_q_a_norm, wuq_pad, mla_qn_norm, gqpe, mla_kv_a_norm, gkpe,
        _rope_tables(pos_s, False), wuk_t, Bd, 1)
    kvw = DA_KV_HEADS * LANES
    oda, olat = _decode(
        page_table,
        qs_da.reshape(Bd, DA_HEADS, LANES), ks_da.reshape(Bd, 1, kvw), vs_da.reshape(Bd, 1, kvw),
        qlat.reshape(Bd, MLA_HEADS, MLA_KV_RANK), qpe.reshape(Bd, MLA_HEADS, LANES),
        ckv_s.reshape(Bd, 1, MLA_KV_RANK), kpe_s.reshape(Bd, 1, MLA_ROPE), lv,
        cache_mla_ckv.reshape(n_phys, page, MLA_KV_RANK),
        jnp.swapaxes(cache_mla_kpe.reshape(n_phys, page, MLA_ROPE), 1, 2),
        cache_diff_k.reshape(n_phys, page * DA_KV_HEADS, LANES),
        cache_diff_v.reshape(n_phys, page * DA_KV_HEADS, LANES),
        _row_tile(n_pages, 32))
    oa_s, ob_s = _sample_post(oda.reshape(Bd, DA_WIDTH), olat.reshape(Bd, MLA_HEADS * MLA_KV_RANK),
                              zs, da_subln, wuv_h)
    y_s = _outproj(oa_s, ob_s, w_o, xs, Bd, 512).reshape(Bd, 1, D)

    return (y_p, y_s,
            k_da.reshape(1, B, S, DA_KV_HEADS, 2 * DA_HEAD_DIM), v_da.reshape(1, B, S, DA_KV_HEADS, DA_V_DIM),
            ckv_p.reshape(1, B, S, MLA_KV_RANK), kpe_p.reshape(1, B, S, MLA_ROPE),
            ks_da.reshape(1, Bd, 1, DA_KV_HEADS, 2 * DA_HEAD_DIM), vs_da.reshape(1, Bd, 1, DA_KV_HEADS, DA_V_DIM),
            ckv_s.reshape(1, Bd, 1, MLA_KV_RANK), kpe_s.reshape(1, Bd, 1, MLA_ROPE))
```

```python
import functools
import math

import jax
import jax.numpy as jnp
from jax import lax
from jax.experimental import pallas as pl
from jax.experimental.pallas import tpu as pltpu

F32 = jnp.float32
BF16 = jnp.bfloat16

EPS = 1e-6
ROPE_THETA = 10000.0
NEG_INF = -1e30
LANES = 128
MIB = 1024 * 1024

DA_HEAD_DIM = 64
DA_V_DIM = 128
DA_HEADS = 16
DA_KV_HEADS = 2
DA_GROUP = DA_HEADS // DA_KV_HEADS
DA_WIDTH = DA_HEADS * DA_V_DIM
DA_SCALE = DA_HEAD_DIM ** -0.5
MLA_HEADS = 16
MLA_NOPE = 128
MLA_ROPE = 64
MLA_V = 128
MLA_KV_RANK = 512
MLA_Q_RANK = 1024
MLA_WIDTH = MLA_HEADS * MLA_V
MLA_SCALE = (MLA_NOPE + MLA_ROPE) ** -0.5
MLA_QK = 2 * LANES
LAM_INIT = 0.8 - 0.6 * math.exp(-0.3 * 0)
LOG2E = math.log2(math.e)

COL_Q = 0
COL_GA = COL_Q + DA_HEADS * 2 * DA_HEAD_DIM
COL_GB = COL_GA + DA_WIDTH
COL_QA = COL_GB + MLA_WIDTH
COL_CKV = COL_QA + MLA_Q_RANK
COL_KV = COL_CKV + MLA_KV_RANK
COL_KPE = COL_KV + 2 * DA_KV_HEADS * DA_V_DIM
INPROJ_TN = 768
N_PAD = COL_KPE + 256
assert N_PAD % INPROJ_TN == 0


def _params(sem, vmem_mib):
    return pltpu.CompilerParams(dimension_semantics=sem, vmem_limit_bytes=vmem_mib * MIB)


def _silu(g):
    return g / (1.0 + jnp.exp(-g))


def _inproj_body(x_ref, g_ref, w_ref, z_ref, h_ref):
    @pl.when(pl.program_id(1) == 0)
    def _():
        x = x_ref[...]
        ms = jnp.mean(x * x, axis=-1, keepdims=True)
        h_ref[...] = (x * lax.rsqrt(ms + EPS) * g_ref[...]).astype(BF16)

    z_ref[...] = lax.dot_general(h_ref[...], w_ref[...], (((1,), (1,)), ((), ())),
                                 preferred_element_type=F32)


def _inproj(x, g, w, tm):
    T, D = x.shape
    N = w.shape[0]
    tn = INPROJ_TN
    return pl.pallas_call(
        _inproj_body,
        grid=(T // tm, N // tn),
        in_specs=[pl.BlockSpec((tm, D), lambda i, j: (i, 0)),
                  pl.BlockSpec((1, D), lambda i, j: (0, 0)),
                  pl.BlockSpec((tn, D), lambda i, j: (j, 0))],
        out_specs=pl.BlockSpec((tm, tn), lambda i, j: (i, j)),
        out_shape=jax.ShapeDtypeStruct((T, N), F32),
        scratch_shapes=[pltpu.VMEM((tm, D), BF16)],
        compiler_params=_params(("parallel", "arbitrary"), 48),
        name="inproj",
    )(x, g, w)


def _chunk_mean_sq(x, m):
    s = x * x
    hi = s.astype(BF16)
    lo = (s - hi.astype(F32)).astype(BF16)
    return (jnp.dot(hi, m, preferred_element_type=F32)
            + jnp.dot(lo, m, preferred_element_type=F32))


def _lane_mean_sq(x, n):
    return jnp.sum(x * x, axis=-1, keepdims=True) * (1.0 / n)


def _rope(y, cos, sa, sb):
    return y * cos + pltpu.roll(y, 96, 1) * sa + pltpu.roll(y, 32, 1) * sb


def _norm_rope(x, m, gain, cos, sa, sb):
    return _rope(x * lax.rsqrt(_chunk_mean_sq(x, m) + EPS) * gain, cos, sa, sb)


def _rope_tables(pos, dual):
    d = DA_HEAD_DIM
    inv = jnp.power(ROPE_THETA, -jnp.arange(0, d, 2, dtype=F32) / d)
    ang = pos.astype(F32)[:, None] * inv[None, :]
    ang = jnp.concatenate([ang, ang], axis=-1)
    cos, sin = jnp.cos(ang), jnp.sin(ang)
    low = jnp.arange(d) < d // 2
    sa = jnp.where(low, -sin, 0.0)
    sb = jnp.where(low, 0.0, sin)
    if dual:
        return tuple(jnp.concatenate([t, t], axis=-1) for t in (cos, sa, sb))
    return tuple(jnp.concatenate([t, jnp.zeros_like(t)], axis=-1) for t in (cos, sa, sb))


def _avg_matrix(chunk, rows):
    i = jnp.arange(LANES)[:, None]
    j = jnp.arange(LANES)[None, :]
    same = (i // chunk == j // chunk) if rows == LANES else (i < rows) & (j >= 0)
    return jnp.where(same, 1.0 / chunk, 0.0).astype(BF16)


def _da_prep_body(qscale, zq_ref, zkv_ref, m_ref, gq_ref, gk_ref, cos_ref, sa_ref, sb_ref,
                  q_ref, k_ref, v_ref, kb_ref, vb_ref):
    m = m_ref[...]
    cos, sa, sb = cos_ref[...], sa_ref[...], sb_ref[...]
    gq, gk = gq_ref[...], gk_ref[...]
    for c in range(DA_HEADS):
        sl = slice(c * LANES, (c + 1) * LANES)
        q_ref[:, sl] = (_norm_rope(zq_ref[:, sl], m, gq, cos, sa, sb) * qscale).astype(BF16)
    for c in range(DA_KV_HEADS):
        sl = slice(c * LANES, (c + 1) * LANES)
        k = _norm_rope(zkv_ref[:, sl], m, gk, cos, sa, sb)
        k_ref[:, sl] = k
        kb_ref[:, sl] = k.astype(BF16)
    v = zkv_ref[:, DA_KV_HEADS * LANES:]
    v_ref[...] = v
    vb_ref[...] = v.astype(BF16)


def _da_prep(z, gq, gk, tables, tm, n_pos_blocks, qscale):
    T = z.shape[0]
    qw = DA_HEADS * LANES
    kw = DA_KV_HEADS * LANES
    row = lambda i: (i, 0)
    const = lambda i: (0, 0)
    tab = lambda i: (i % n_pos_blocks, 0)
    return pl.pallas_call(
        functools.partial(_da_prep_body, qscale),
        grid=(T // tm,),
        in_specs=[pl.BlockSpec((tm, qw), lambda i: (i, COL_Q // qw)),
                  pl.BlockSpec((tm, 2 * kw), lambda i: (i, COL_KV // (2 * kw))),
                  pl.BlockSpec((LANES, LANES), const),
                  pl.BlockSpec((1, LANES), const),
                  pl.BlockSpec((1, LANES), const),
                  pl.BlockSpec((tm, LANES), tab),
                  pl.BlockSpec((tm, LANES), tab),
                  pl.BlockSpec((tm, LANES), tab)],
        out_specs=[pl.BlockSpec((tm, qw), row),
                   pl.BlockSpec((tm, kw), row),
                   pl.BlockSpec((tm, kw), row),
                   pl.BlockSpec((tm, kw), row),
                   pl.BlockSpec((tm, kw), row)],
        out_shape=[jax.ShapeDtypeStruct((T, qw), BF16),
                   jax.ShapeDtypeStruct((T, kw), F32),
                   jax.ShapeDtypeStruct((T, kw), F32),
                   jax.ShapeDtypeStruct((T, kw), BF16),
                   jax.ShapeDtypeStruct((T, kw), BF16)],
        compiler_params=_params(("parallel",), 40),
        name="da_prep",
    )(z, z, _avg_matrix(DA_HEAD_DIM, LANES), gq, gk, *tables)


def _mla_prep_body(prompt, zqa_ref, zckv_ref, zkpe_ref, gqa_ref, wuq_ref,
                   gqn_ref, gqpe_ref, gckv_ref, gkpe_ref, cos_ref, sa_ref, sb_ref, wx_ref,
                   ckv_ref, kpe_ref, *outs):
    cos, sa, sb = cos_ref[...], sa_ref[...], sb_ref[...]
    gqn, gqpe = gqn_ref[...], gqpe_ref[...]

    xa = zqa_ref[...]
    qa = (xa * lax.rsqrt(jnp.mean(xa * xa, axis=-1, keepdims=True) + EPS) * gqa_ref[...]).astype(BF16)
    xc = zckv_ref[...]
    ckv = xc * lax.rsqrt(jnp.mean(xc * xc, axis=-1, keepdims=True) + EPS) * gckv_ref[...]
    xk = zkpe_ref[...]
    kpe = _rope(xk * lax.rsqrt(_lane_mean_sq(xk, MLA_ROPE) + EPS) * gkpe_ref[...], cos, sa, sb)
    ckv_ref[...] = ckv
    kpe_ref[...] = kpe[:, :MLA_ROPE]

    for h in range(MLA_HEADS):
        qm = jnp.dot(qa, wuq_ref[:, h * MLA_QK:(h + 1) * MLA_QK], preferred_element_type=F32)
        qn = qm[:, :LANES]
        qn = qn * lax.rsqrt(_lane_mean_sq(qn, MLA_NOPE) + EPS) * gqn
        qp = qm[:, LANES:]
        qp = _rope(qp * lax.rsqrt(_lane_mean_sq(qp, MLA_ROPE) + EPS) * gqpe, cos, sa, sb)
        if prompt:
            q_ref = outs[0]
            q_ref[:, h * MLA_QK:h * MLA_QK + LANES] = (qn * (MLA_SCALE * LOG2E)).astype(BF16)
            q_ref[:, h * MLA_QK + LANES:(h + 1) * MLA_QK] = (qp * (MLA_SCALE * LOG2E)).astype(BF16)
        else:
            qlat_ref, qpe_ref = outs
            qlat = jnp.dot(qn.astype(BF16), wx_ref[h], preferred_element_type=F32)
            qlat_ref[:, h * MLA_KV_RANK:(h + 1) * MLA_KV_RANK] = (qlat * MLA_SCALE).astype(BF16)
            qpe_ref[:, h * LANES:(h + 1) * LANES] = (qp * MLA_SCALE).astype(BF16)

    if prompt:
        _, k_ref, v_ref = outs
        kv = jnp.dot(ckv.astype(BF16), wx_ref[...], preferred_element_type=F32)
        kpb = kpe.astype(BF16)
        for h in range(MLA_HEADS):
            k_ref[:, h * MLA_QK:h * MLA_QK + LANES] = kv[:, h * LANES:(h + 1) * LANES].astype(BF16)
            k_ref[:, h * MLA_QK + LANES:(h + 1) * MLA_QK] = kpb
        v_ref[...] = kv[:, MLA_HEADS * MLA_NOPE:].astype(BF16)


def _mla_prep(z, prompt, gqa, wuq, gqn, gqpe, gckv, gkpe, tables, wx, tm, n_pos_blocks):
    T = z.shape[0]
    row = lambda i: (i, 0)
    const = lambda i: (0, 0)
    tab = lambda i: (i % n_pos_blocks, 0)
    once = pl.Buffered(1)
    if prompt:
        wx_spec = pl.BlockSpec(wx.shape, const, pipeline_mode=once)
        extra_specs = [pl.BlockSpec((tm, MLA_HEADS * MLA_QK), row),
                       pl.BlockSpec((tm, MLA_HEADS * MLA_QK), row),
                       pl.BlockSpec((tm, MLA_WIDTH), row)]
        extra_shapes = [jax.ShapeDtypeStruct((T, MLA_HEADS * MLA_QK), BF16),
                        jax.ShapeDtypeStruct((T, MLA_HEADS * MLA_QK), BF16),
                        jax.ShapeDtypeStruct((T, MLA_WIDTH), BF16)]
    else:
        wx_spec = pl.BlockSpec(wx.shape, lambda i: (0, 0, 0), pipeline_mode=once)
        extra_specs = [pl.BlockSpec((tm, MLA_HEADS * MLA_KV_RANK), row),
                       pl.BlockSpec((tm, MLA_HEADS * LANES), row)]
        extra_shapes = [jax.ShapeDtypeStruct((T, MLA_HEADS * MLA_KV_RANK), BF16),
                        jax.ShapeDtypeStruct((T, MLA_HEADS * LANES), BF16)]
    return pl.pallas_call(
        functools.partial(_mla_prep_body, prompt),
        grid=(T // tm,),
        in_specs=[pl.BlockSpec((tm, MLA_Q_RANK), lambda i: (i, COL_QA // MLA_Q_RANK)),
                  pl.BlockSpec((tm, MLA_KV_RANK), lambda i: (i, COL_CKV // MLA_KV_RANK)),
                  pl.BlockSpec((tm, LANES), lambda i: (i, COL_KPE // LANES)),
                  pl.BlockSpec((1, MLA_Q_RANK), const),
                  pl.BlockSpec(wuq.shape, const, pipeline_mode=once),
                  pl.BlockSpec((1, LANES), const),
                  pl.BlockSpec((1, LANES), const),
                  pl.BlockSpec((1, MLA_KV_RANK), const),
                  pl.BlockSpec((1, LANES), const),
                  pl.BlockSpec((tm, LANES), tab),
                  pl.BlockSpec((tm, LANES), tab),
                  pl.BlockSpec((tm, LANES), tab),
                  wx_spec],
        out_specs=[pl.BlockSpec((tm, MLA_KV_RANK), row),
                   pl.BlockSpec((tm, MLA_ROPE), row)] + extra_specs,
        out_shape=[jax.ShapeDtypeStruct((T, MLA_KV_RANK), F32),
                   jax.ShapeDtypeStruct((T, MLA_ROPE), F32)] + extra_shapes,
        compiler_params=_params(("parallel",), 56),
        name="mla_prep_prompt" if prompt else "mla_prep_sample",
    )(z, z, z, gqa, wuq, gqn, gqpe, gckv, gkpe, *tables, wx)


def _da_lambda(lv):
    e1 = jnp.exp(jnp.sum(lv[0:1] * lv[1:2], axis=-1, keepdims=True))
    e2 = jnp.exp(jnp.sum(lv[2:3] * lv[3:4], axis=-1, keepdims=True))
    return e1 - e2 + LAM_INIT


def _softmax_step_t(s, m_ref, l_ref, acc_ref, idx, v_t):
    m_all, l_all = m_ref[idx], l_ref[idx]
    ms, ls, ps, corrs = [], [], [], []
    for j in range(s.shape[1] // LANES):
        cs = slice(j * LANES, (j + 1) * LANES)
        sj = s[:, cs]
        m_prev = m_all[:, cs]
        m_new = jnp.maximum(m_prev, jnp.max(sj, axis=0, keepdims=True))
        corr = jnp.exp2(m_prev - m_new)
        p = jnp.exp2(sj - m_new)
        ls.append(l_all[:, cs] * corr + jnp.sum(p, axis=0, keepdims=True))
        ms.append(m_new)
        ps.append(p.astype(BF16))
        corrs.append(corr)
    m_ref[idx] = jnp.concatenate(ms, axis=1)
    l_ref[idx] = jnp.concatenate(ls, axis=1)
    p = jnp.concatenate(ps, axis=1)
    corr = jnp.concatenate(corrs, axis=1)
    acc_ref[idx] = acc_ref[idx] * corr + jnp.dot(v_t, p, preferred_element_type=F32)


def _da_attn_body(qt_ref, k_ref, vt_ref, g_ref, lv_ref, gs_ref, o_ref, qbd_ref, m_ref, l_ref, acc_ref, *, t):
    qi = pl.program_id(2)
    cols = 2 * DA_GROUP * t
    low = lax.broadcasted_iota(jnp.int32, (LANES, 1), 0) < DA_HEAD_DIM
    for r in range(DA_GROUP):
        q = qt_ref[r * LANES:(r + 1) * LANES, :]
        zero = jnp.zeros_like(q)
        qbd_ref[:, (2 * r) * t:(2 * r + 1) * t] = jnp.where(low, q, zero)
        qbd_ref[:, (2 * r + 1) * t:(2 * r + 2) * t] = jnp.where(low, zero, q)
    m_ref[...] = jnp.full(m_ref.shape, NEG_INF, F32)
    l_ref[...] = jnp.zeros(l_ref.shape, F32)
    acc_ref[...] = jnp.zeros(acc_ref.shape, F32)

    def step(ki, masked):
        k = k_ref[pl.ds(pl.multiple_of(ki * t, t), t), :]
        s = jnp.dot(k, qbd_ref[...], preferred_element_type=F32)
        if masked:
            key = lax.broadcasted_iota(jnp.int32, (t, cols), 0)
            qry = lax.broadcasted_iota(jnp.int32, (t, cols), 1) % t
            s = jnp.where(key <= qry, s, NEG_INF)
        _softmax_step_t(s, m_ref, l_ref, acc_ref, 0, vt_ref[0, ki])

    def loop_body(ki, carry):
        step(ki, False)
        return carry

    lax.fori_loop(0, qi, loop_body, 0)
    step(qi, True)

    lam = _da_lambda(lv_ref[...])
    gs = gs_ref[...] * (1.0 - LAM_INIT)
    inv_l = 1.0 / l_ref[0]
    for r in range(DA_GROUP):
        a1 = slice((2 * r) * t, (2 * r + 1) * t)
        a2 = slice((2 * r + 1) * t, (2 * r + 2) * t)
        od = acc_ref[0, :, a1] * inv_l[:, a1] - lam * (acc_ref[0, :, a2] * inv_l[:, a2])
        y = (od * lax.rsqrt(jnp.mean(od * od, axis=0, keepdims=True) + EPS)).T
        sl = slice(r * LANES, (r + 1) * LANES)
        o_ref[:, sl] = (y * gs * _silu(g_ref[:, sl])).astype(BF16)


def _da_attn(qt, kb, vt, z, lv, gs, B, S, t):
    T = kb.shape[0]
    nq = S // t
    gw = DA_GROUP * LANES
    cols = 2 * DA_GROUP * t
    return pl.pallas_call(
        functools.partial(_da_attn_body, t=t),
        grid=(B, DA_KV_HEADS, nq),
        in_specs=[pl.BlockSpec((gw, t), lambda b, g, i: (g, b * nq + i)),
                  pl.BlockSpec((S, LANES), lambda b, g, i: (b, g)),
                  pl.BlockSpec((1, nq, DA_V_DIM, t), lambda b, g, i: (g, b, 0, 0)),
                  pl.BlockSpec((t, gw), lambda b, g, i: (b * nq + i, COL_GA // gw + g)),
                  pl.BlockSpec((4, DA_HEAD_DIM), lambda b, g, i: (0, 0)),
                  pl.BlockSpec((1, LANES), lambda b, g, i: (0, 0))],
        out_specs=pl.BlockSpec((t, gw), lambda b, g, i: (b * nq + i, g)),
        out_shape=jax.ShapeDtypeStruct((T, DA_WIDTH), BF16),
        scratch_shapes=[pltpu.VMEM((LANES, cols), BF16),
                        pltpu.VMEM((1, 1, cols), F32),
                        pltpu.VMEM((1, 1, cols), F32),
                        pltpu.VMEM((1, DA_V_DIM, cols), F32)],
        compiler_params=_params(("parallel", "parallel", "arbitrary"), 48),
        name="da_attn",
    )(qt, kb, vt, z, lv, gs)


def _mla_attn_body(qt_ref, k_ref, vt_ref, g_ref, o_ref, m_ref, l_ref, acc_ref, *, t, hb):
    qi = pl.program_id(2)
    m_ref[...] = jnp.full(m_ref.shape, NEG_INF, F32)
    l_ref[...] = jnp.zeros(l_ref.shape, F32)
    acc_ref[...] = jnp.zeros(acc_ref.shape, F32)

    def step(ki, masked):
        row0 = pl.multiple_of(ki * t, t)
        for h in range(hb):
            k = k_ref[pl.ds(row0, t), h * MLA_QK:(h + 1) * MLA_QK]
            s = jnp.dot(k, qt_ref[h * MLA_QK:(h + 1) * MLA_QK, :], preferred_element_type=F32)
            if masked:
                key = lax.broadcasted_iota(jnp.int32, (t, t), 0)
                qry = lax.broadcasted_iota(jnp.int32, (t, t), 1)
                s = jnp.where(key <= qry, s, NEG_INF)
            _softmax_step_t(s, m_ref, l_ref, acc_ref, h, vt_ref[h, ki])

    def loop_body(ki, carry):
        step(ki, False)
        return carry

    lax.fori_loop(0, qi, loop_body, 0)
    step(qi, True)

    for h in range(hb):
        sl = slice(h * MLA_V, (h + 1) * MLA_V)
        o = (acc_ref[h] * (1.0 / l_ref[h])).T
        o_ref[:, sl] = (o * _silu(g_ref[:, sl])).astype(BF16)


def _mla_attn(qt, k, vt, z, B, S, t, hb):
    T = k.shape[0]
    nq = S // t
    return pl.pallas_call(
        functools.partial(_mla_attn_body, t=t, hb=hb),
        grid=(B, MLA_HEADS // hb, nq),
        in_specs=[pl.BlockSpec((hb * MLA_QK, t), lambda b, h, i: (h, b * nq + i)),
                  pl.BlockSpec((S, hb * MLA_QK), lambda b, h, i: (b, h)),
                  pl.BlockSpec((hb, nq, MLA_V, t), lambda b, h, i: (h, b, 0, 0)),
                  pl.BlockSpec((t, hb * MLA_V), lambda b, h, i: (b * nq + i, COL_GB // (hb * MLA_V) + h))],
        out_specs=pl.BlockSpec((t, hb * MLA_V), lambda b, h, i: (b * nq + i, h)),
        out_shape=jax.ShapeDtypeStruct((T, MLA_WIDTH), BF16),
        scratch_shapes=[pltpu.VMEM((hb, 1, t), F32),
                        pltpu.VMEM((hb, 1, t), F32),
                        pltpu.VMEM((hb, MLA_V, t), F32)],
        compiler_params=_params(("parallel", "parallel", "arbitrary"), 48),
        name="mla_attn",
    )(qt, k, vt, z)


def _online_update(m_ref, l_ref, acc_ref, s, pv_fn):
    m_prev = m_ref[...]
    m_new = jnp.maximum(m_prev, jnp.max(s, axis=-1, keepdims=True))
    corr = jnp.exp(m_prev - m_new)
    p = jnp.exp(s - m_new)
    l_ref[...] = l_ref[...] * corr + jnp.sum(p, axis=-1, keepdims=True)
    acc_ref[...] = acc_ref[...] * corr + pv_fn(p)
    m_ref[...] = m_new


def _decode_body(pt_ref, qda_ref, kn_ref, vn_ref, qlat_ref, qpe_ref, ckvn_ref, kpen_ref, lv_ref,
                 ck_hbm, kp_hbm, dk_hbm, dv_hbm, oda_ref, olat_ref,
                 ck_buf, kp_buf, dk_buf, dv_buf, sem,
                 m_ml, l_ml, acc_ml, m_da, l_da, acc_da, *, P, NC, page):
    n = pl.program_id(0)
    c = n % NC
    slot = n % 2
    nt = (((1,), (1,)), ((), ()))

    def copies(n_, slot_, real):
        b_ = n_ // NC
        c_ = n_ % NC
        out = []
        for j in range(P):
            pg = pt_ref[b_, c_ * P + j] if real else 0
            out.append(pltpu.make_async_copy(
                ck_hbm.at[pg], ck_buf.at[slot_, pl.ds(j * page, page), :], sem.at[0, slot_]))
            out.append(pltpu.make_async_copy(
                kp_hbm.at[pg], kp_buf.at[slot_, :, pl.ds(j * page, page)], sem.at[1, slot_]))
            out.append(pltpu.make_async_copy(
                dk_hbm.at[pg], dk_buf.at[slot_, pl.ds(j * 2 * page, 2 * page), :], sem.at[2, slot_]))
            out.append(pltpu.make_async_copy(
                dv_hbm.at[pg], dv_buf.at[slot_, pl.ds(j * 2 * page, 2 * page), :], sem.at[3, slot_]))
        return out

    @pl.when(n == 0)
    def _():
        for cp in copies(0, 0, True):
            cp.start()

    @pl.when(n + 1 < pl.num_programs(0))
    def _():
        for cp in copies(n + 1, 1 - slot, True):
            cp.start()

    for cp in copies(n, slot, False):
        cp.wait()

    @pl.when(c == 0)
    def _():
        for m_ref, l_ref, acc_ref in ((m_ml, l_ml, acc_ml), (m_da, l_da, acc_da)):
            m_ref[...] = jnp.full(m_ref.shape, NEG_INF, F32)
            l_ref[...] = jnp.zeros(l_ref.shape, F32)
            acc_ref[...] = jnp.zeros(acc_ref.shape, F32)

    ck = ck_buf[slot].astype(BF16)
    kp = kp_buf[slot].astype(BF16)
    ql = qlat_ref[0]
    qp = qpe_ref[0][:, :MLA_ROPE]
    s = (lax.dot_general(ql, ck, nt, preferred_element_type=F32)
         + jnp.dot(qp, kp, preferred_element_type=F32))
    _online_update(m_ml, l_ml, acc_ml, s,
                   lambda p: jnp.dot(p.astype(BF16), ck, preferred_element_type=F32))

    low = lax.broadcasted_iota(jnp.int32, (1, LANES), 1) < DA_HEAD_DIM
    qda = qda_ref[0].astype(F32)
    parts = []
    for g in range(DA_KV_HEADS):
        q = qda[g * DA_GROUP:(g + 1) * DA_GROUP, :]
        zero = jnp.zeros_like(q)
        parts += [jnp.where(low, q, zero), jnp.where(low, zero, q)]
    q32 = jnp.concatenate(parts, axis=0)
    dk = dk_buf[slot].astype(BF16)
    dv = dv_buf[slot].astype(BF16)
    s = lax.dot_general(q32.astype(BF16), dk, nt, preferred_element_type=F32)
    col_head = lax.broadcasted_iota(jnp.int32, s.shape, 1) % DA_KV_HEADS
    row_head = lax.broadcasted_iota(jnp.int32, s.shape, 0) // (2 * DA_GROUP)
    s = jnp.where(col_head == row_head, s, NEG_INF)
    _online_update(m_da, l_da, acc_da, s,
                   lambda p: jnp.dot(p.astype(BF16), dv, preferred_element_type=F32))

    @pl.when(c == NC - 1)
    def _():
        ckvn = ckvn_ref[0]
        s = (jnp.sum(ql.astype(F32) * ckvn, axis=-1, keepdims=True)
             + jnp.sum(qp.astype(F32) * kpen_ref[0], axis=-1, keepdims=True))
        _online_update(m_ml, l_ml, acc_ml, s, lambda p: p * ckvn)
        olat_ref[0] = acc_ml[...] / l_ml[...]

        per_head = lambda x: jnp.concatenate(
            [jnp.broadcast_to(x[:, g * LANES:(g + 1) * LANES], (2 * DA_GROUP, LANES))
             for g in range(DA_KV_HEADS)], axis=0)
        kn = per_head(kn_ref[0])
        vn = per_head(vn_ref[0])
        s = jnp.sum(q32 * kn, axis=-1, keepdims=True)
        _online_update(m_da, l_da, acc_da, s, lambda p: p * vn)
        o = acc_da[...] / l_da[...]
        lam = _da_lambda(lv_ref[...])
        for g in range(DA_KV_HEADS):
            r0 = g * 2 * DA_GROUP
            oda_ref[0, g * DA_GROUP:(g + 1) * DA_GROUP, :] = (
                o[r0:r0 + DA_GROUP] - lam * o[r0 + DA_GROUP:r0 + 2 * DA_GROUP])


def _decode(page_table, qda, kn, vn, qlat, qpe, ckvn, kpen, lv, ck, kp, dk, dv, P):
    Bd, n_pages = page_table.shape
    page = ck.shape[1]
    NC = n_pages // P
    rows_da = 2 * DA_KV_HEADS * DA_GROUP
    blk = lambda n, pt: (n // NC, 0, 0)
    any_spec = pl.BlockSpec(memory_space=pl.ANY)
    grid_spec = pltpu.PrefetchScalarGridSpec(
        num_scalar_prefetch=1,
        grid=(Bd * NC,),
        in_specs=[pl.BlockSpec((1, DA_HEADS, LANES), blk),
                  pl.BlockSpec((1, 1, DA_KV_HEADS * LANES), blk),
                  pl.BlockSpec((1, 1, DA_KV_HEADS * LANES), blk),
                  pl.BlockSpec((1, MLA_HEADS, MLA_KV_RANK), blk),
                  pl.BlockSpec((1, MLA_HEADS, LANES), blk),
                  pl.BlockSpec((1, 1, MLA_KV_RANK), blk),
                  pl.BlockSpec((1, 1, MLA_ROPE), blk),
                  pl.BlockSpec((4, DA_HEAD_DIM), lambda n, pt: (0, 0)),
                  any_spec, any_spec, any_spec, any_spec],
        out_specs=[pl.BlockSpec((1, DA_HEADS, LANES), blk),
                   pl.BlockSpec((1, MLA_HEADS, MLA_KV_RANK), blk)],
        scratch_shapes=[pltpu.VMEM((2, P * page, MLA_KV_RANK), F32),
                        pltpu.VMEM((2, MLA_ROPE, P * page), F32),
                        pltpu.VMEM((2, P * page * DA_KV_HEADS, LANES), F32),
                        pltpu.VMEM((2, P * page * DA_KV_HEADS, LANES), F32),
                        pltpu.SemaphoreType.DMA((4, 2)),
                        pltpu.VMEM((MLA_HEADS, 1), F32),
                        pltpu.VMEM((MLA_HEADS, 1), F32),
                        pltpu.VMEM((MLA_HEADS, MLA_KV_RANK), F32),
                        pltpu.VMEM((rows_da, 1), F32),
                        pltpu.VMEM((rows_da, 1), F32),
                        pltpu.VMEM((rows_da, LANES), F32)])
    return pl.pallas_call(
        functools.partial(_decode_body, P=P, NC=NC, page=page),
        grid_spec=grid_spec,
        out_shape=[jax.ShapeDtypeStruct((Bd, DA_HEADS, LANES), F32),
                   jax.ShapeDtypeStruct((Bd, MLA_HEADS, MLA_KV_RANK), F32)],
        compiler_params=_params(("arbitrary",), 56),
        name="decode",
    )(page_table, qda, kn, vn, qlat, qpe, ckvn, kpen, lv, ck, kp, dk, dv)


def _sample_post_body(oda_ref, olat_ref, ga_ref, gb_ref, gs_ref, wuv_ref, oa_ref, ob_ref):
    gs = gs_ref[...]
    for h in range(DA_HEADS):
        sl = slice(h * LANES, (h + 1) * LANES)
        od = oda_ref[:, sl]
        y = od * lax.rsqrt(jnp.mean(od * od, axis=-1, keepdims=True) + EPS) * gs * (1.0 - LAM_INIT)
        oa_ref[:, sl] = (y * _silu(ga_ref[:, sl])).astype(BF16)
    for h in range(MLA_HEADS):
        sl = slice(h * MLA_V, (h + 1) * MLA_V)
        ol = olat_ref[:, h * MLA_KV_RANK:(h + 1) * MLA_KV_RANK].astype(BF16)
        ob = jnp.dot(ol, wuv_ref[h], preferred_element_type=F32)
        ob_ref[:, sl] = (ob * _silu(gb_ref[:, sl])).astype(BF16)


def _sample_post(oda, olat, z, gs, wuv):
    Bd = oda.shape[0]
    const = lambda i: (0, 0)
    return pl.pallas_call(
        _sample_post_body,
        grid=(1,),
        in_specs=[pl.BlockSpec((Bd, DA_WIDTH), const),
                  pl.BlockSpec((Bd, MLA_HEADS * MLA_KV_RANK), const),
                  pl.BlockSpec((Bd, DA_WIDTH), lambda i: (0, COL_GA // DA_WIDTH)),
                  pl.BlockSpec((Bd, MLA_WIDTH), lambda i: (0, COL_GB // MLA_WIDTH)),
                  pl.BlockSpec((1, LANES), const),
                  pl.BlockSpec(wuv.shape, lambda i: (0, 0, 0))],
        out_specs=[pl.BlockSpec((Bd, DA_WIDTH), const),
                   pl.BlockSpec((Bd, MLA_WIDTH), const)],
        out_shape=[jax.ShapeDtypeStruct((Bd, DA_WIDTH), BF16),
                   jax.ShapeDtypeStruct((Bd, MLA_WIDTH), BF16)],
        compiler_params=_params(("arbitrary",), 40),
        name="sample_post",
    )(oda, olat, z, z, gs, wuv)


def _outproj_body(oa_ref, ob_ref, w_ref, x_ref, y_ref):
    ka = oa_ref.shape[1]
    y_ref[...] = (x_ref[...]
                  + jnp.dot(oa_ref[...], w_ref[:ka, :], preferred_element_type=F32)
                  + jnp.dot(ob_ref[...], w_ref[ka:, :], preferred_element_type=F32))


def _outproj(oa, ob, w, x, tm, tn):
    T, D = x.shape
    return pl.pallas_call(
        _outproj_body,
        grid=(T // tm, D // tn),
        in_specs=[pl.BlockSpec((tm, oa.shape[1]), lambda i, j: (i, 0)),
                  pl.BlockSpec((tm, ob.shape[1]), lambda i, j: (i, 0)),
                  pl.BlockSpec((w.shape[0], tn), lambda i, j: (0, j)),
                  pl.BlockSpec((tm, tn), lambda i, j: (i, j))],
        out_specs=pl.BlockSpec((tm, tn), lambda i, j: (i, j)),
        out_shape=jax.ShapeDtypeStruct((T, D), F32),
        compiler_params=_params(("parallel", "arbitrary"), 48),
        name="outproj",
    )(oa, ob, w, x)


def _row_tile(T, cap):
    t = min(T, cap)
    while T % t:
        t //= 2
    return t


def kernel(x_prompt, x_sample, cache_diff_k, cache_diff_v, cache_mla_ckv, cache_mla_kpe, page_table,
           g_pre, w_in, da_q_norm, da_k_norm, da_lambda_q1, da_lambda_k1, da_lambda_q2, da_lambda_k2,
           da_subln, mla_q_a_norm, w_uq, mla_qn_norm, mla_qpe_norm, mla_kv_a_norm, mla_kpe_norm,
           w_uk, w_uv, w_out):
    B, S, D = x_prompt.shape
    Bd, Td, _ = x_sample.shape
    depth, n_phys, page = cache_mla_ckv.shape[:3]
    n_pages = page_table.shape[1]
    assert depth == 1 and Td == 1
    assert w_uq.shape[1:] == (MLA_Q_RANK, MLA_HEADS, MLA_NOPE + MLA_ROPE)
    assert w_uk.shape[1:] == (MLA_KV_RANK, MLA_HEADS, MLA_NOPE)
    assert w_uv.shape[1:] == (MLA_KV_RANK, MLA_HEADS, MLA_V)
    assert cache_diff_k.shape[3:] == (DA_KV_HEADS, 2 * DA_HEAD_DIM)
    assert w_in.shape[2] == COL_KPE + MLA_ROPE

    w = w_in[0].T
    sizes = (DA_HEADS * 2 * DA_HEAD_DIM, DA_KV_HEADS * 2 * DA_HEAD_DIM, DA_KV_HEADS * DA_V_DIM, DA_WIDTH,
             MLA_Q_RANK, MLA_KV_RANK, MLA_ROPE, MLA_WIDTH)
    offs = [0]
    for s_ in sizes:
        offs.append(offs[-1] + s_)
    sec = [w[offs[i]:offs[i + 1]] for i in range(len(sizes))]
    zq, zk, zv, ga, zqa, zckv, zkpe, gb = sec
    w_perm = jnp.concatenate(
        [s_.astype(BF16) for s_ in (zq, ga, gb, zqa, zckv, zk, zv, zkpe)]
        + [jnp.zeros((N_PAD - COL_KPE - MLA_ROPE, D), BF16)], axis=0)
    wuq = w_uq[0]
    wuq_pad = jnp.concatenate(
        [wuq, jnp.zeros((MLA_Q_RANK, MLA_HEADS, MLA_QK - MLA_NOPE - MLA_ROPE), wuq.dtype)], axis=-1)
    wuq_pad = wuq_pad.reshape(MLA_Q_RANK, MLA_HEADS * MLA_QK).astype(BF16)
    wkv = jnp.concatenate([w_uk[0].reshape(MLA_KV_RANK, -1), w_uv[0].reshape(MLA_KV_RANK, -1)],
                          axis=1).astype(BF16)
    wuk_t = jnp.transpose(w_uk[0], (1, 2, 0)).astype(BF16)
    wuv_h = jnp.transpose(w_uv[0], (1, 0, 2)).astype(BF16)
    w_o = w_out[0].astype(BF16)

    tile2 = lambda g: jnp.concatenate([g, g], axis=-1)
    pad2 = lambda g: jnp.concatenate([g, jnp.zeros_like(g)], axis=-1)
    gq, gk = tile2(da_q_norm), tile2(da_k_norm)
    gqpe, gkpe = pad2(mla_qpe_norm), pad2(mla_kpe_norm)
    lv = jnp.concatenate([da_lambda_q1, da_lambda_k1, da_lambda_q2, da_lambda_k2], axis=0)

    Tp = B * S
    xp = x_prompt.reshape(Tp, D)
    tm = _row_tile(S, 512)
    pos_p = jnp.arange(S)
    z = _inproj(xp, g_pre, w_perm, tm)
    q_da, k_da, v_da, kb_da, vb_da = _da_prep(z, gq, gk, _rope_tables(pos_p, True), tm, S // tm,
                                              DA_SCALE * LOG2E)
    tm_ml = _row_tile(S, 512)
    ckv_p, kpe_p, q_ml, k_ml, v_ml = _mla_prep(
        z, True, mla_q_a_norm, wuq_pad, mla_qn_norm, gqpe, mla_kv_a_norm, gkpe,
        _rope_tables(pos_p, False), wkv, tm_ml, S // tm_ml)
    t_da = _row_tile(S, 256)
    t_ml = _row_tile(S, 512)
    key_tiles_t = lambda v, t, heads, dim: jnp.transpose(
        v.reshape(Tp // t, t, heads, dim), (2, 0, 3, 1))
    o_a = _da_attn(q_da.T, kb_da, key_tiles_t(vb_da, t_da, DA_KV_HEADS, DA_V_DIM), z, lv, da_subln,
                   B, S, t_da)
    o_b = _mla_attn(q_ml.T, k_ml, key_tiles_t(v_ml, t_ml, MLA_HEADS, MLA_V), z, B, S, t_ml, 4)
    y_p = _outproj(o_a, o_b, w_o, xp, _row_tile(Tp, 1024), 512).reshape(B, S, D)

    xs = x_sample.reshape(Bd, D)
    pos_s = jnp.full((Bd,), n_pages * page, jnp.int32)
    zs = _inproj(xs, g_pre, w_perm, Bd)
    qs_da, ks_da, vs_da, _, _ = _da_prep(zs, gq, gk, _rope_tables(pos_s, True), Bd, 1, DA_SCALE)
    ckv_s, kpe_s, qlat, qpe = _mla_prep(
        zs, False, mla_q_a_norm, wuq_pad, mla_qn_norm, gqpe, mla_kv_a_norm, gkpe,
        _rope_tables(pos_s, False), wuk_t, Bd, 1)
    kvw = DA_KV_HEADS * LANES
    oda, olat = _decode(
        page_table,
        qs_da.reshape(Bd, DA_HEADS, LANES), ks_da.reshape(Bd, 1, kvw), vs_da.reshape(Bd, 1, kvw),
        qlat.reshape(Bd, MLA_HEADS, MLA_KV_RANK), qpe.reshape(Bd, MLA_HEADS, LANES),
        ckv_s.reshape(Bd, 1, MLA_KV_RANK), kpe_s.reshape(Bd, 1, MLA_ROPE), lv,
        cache_mla_ckv.reshape(n_phys, page, MLA_KV_RANK),
        jnp.swapaxes(cache_mla_kpe.reshape(n_phys, page, MLA_ROPE), 1, 2),
        cache_diff_k.reshape(n_phys, page * DA_KV_HEADS, LANES),
        cache_diff_v.reshape(n_phys, page * DA_KV_HEADS, LANES),
        _row_tile(n_pages, 32))
    oa_s, ob_s = _sample_post(oda.reshape(Bd, DA_WIDTH), olat.reshape(Bd, MLA_HEADS * MLA_KV_RANK),
                              zs, da_subln, wuv_h)
    y_s = _outproj(oa_s, ob_s, w_o, xs, Bd, 512).reshape(Bd, 1, D)

    return (y_p, y_s,
            k_da.reshape(1, B, S, DA_KV_HEADS, 2 * DA_HEAD_DIM), v_da.reshape(1, B, S, DA_KV_HEADS, DA_V_DIM),
            ckv_p.reshape(1, B, S, MLA_KV_RANK), kpe_p.reshape(1, B, S, MLA_ROPE),
            ks_da.reshape(1, Bd, 1, DA_KV_HEADS, 2 * DA_HEAD_DIM), vs_da.reshape(1, Bd, 1, DA_KV_HEADS, DA_V_DIM),
            ckv_s.reshape(1, Bd, 1, MLA_KV_RANK), kpe_s.reshape(1, Bd, 1, MLA_ROPE))
```

```python
import functools
import math

import jax
import jax.numpy as jnp
from jax import lax
from jax.experimental import pallas as pl
from jax.experimental.pallas import tpu as pltpu

F32 = jnp.float32
BF16 = jnp.bfloat16

EPS = 1e-6
ROPE_THETA = 10000.0
NEG_INF = -1e30
LANES = 128
MIB = 1024 * 1024

DA_HEAD_DIM = 64
DA_V_DIM = 128
DA_HEADS = 16
DA_KV_HEADS = 2
DA_GROUP = DA_HEADS // DA_KV_HEADS
DA_WIDTH = DA_HEADS * DA_V_DIM
DA_SCALE = DA_HEAD_DIM ** -0.5
MLA_HEADS = 16
MLA_NOPE = 128
MLA_ROPE = 64
MLA_V = 128
MLA_KV_RANK = 512
MLA_Q_RANK = 1024
MLA_WIDTH = MLA_HEADS * MLA_V
MLA_SCALE = (MLA_NOPE + MLA_ROPE) ** -0.5
MLA_QK = 2 * LANES
LAM_INIT = 0.8 - 0.6 * math.exp(-0.3 * 0)
LOG2E = math.log2(math.e)

COL_Q = 0
COL_GA = COL_Q + DA_HEADS * 2 * DA_HEAD_DIM
COL_GB = COL_GA + DA_WIDTH
COL_QA = COL_GB + MLA_WIDTH
COL_CKV = COL_QA + MLA_Q_RANK
COL_KV = COL_CKV + MLA_KV_RANK
COL_KPE = COL_KV + 2 * DA_KV_HEADS * DA_V_DIM
INPROJ_TN = 768
N_PAD = COL_KPE + 256
assert N_PAD % INPROJ_TN == 0


def _params(sem, vmem_mib):
    return pltpu.CompilerParams(dimension_semantics=sem, vmem_limit_bytes=vmem_mib * MIB)


def _silu(g):
    return g / (1.0 + jnp.exp(-g))


def _inproj_body(x_ref, g_ref, w_ref, z_ref, h_ref):
    @pl.when(pl.program_id(1) == 0)
    def _():
        x = x_ref[...]
        ms = jnp.mean(x * x, axis=-1, keepdims=True)
        h_ref[...] = (x * lax.rsqrt(ms + EPS) * g_ref[...]).astype(BF16)

    z_ref[...] = lax.dot_general(h_ref[...], w_ref[...], (((1,), (1,)), ((), ())),
                                 preferred_element_type=F32)


def _inproj(x, g, w, tm):
    T, D = x.shape
    N = w.shape[0]
    tn = INPROJ_TN
    return pl.pallas_call(
        _inproj_body,
        grid=(T // tm, N // tn),
        in_specs=[pl.BlockSpec((tm, D), lambda i, j: (i, 0)),
                  pl.BlockSpec((1, D), lambda i, j: (0, 0)),
                  pl.BlockSpec((tn, D), lambda i, j: (j, 0))],
        out_specs=pl.BlockSpec((tm, tn), lambda i, j: (i, j)),
        out_shape=jax.ShapeDtypeStruct((T, N), F32),
        scratch_shapes=[pltpu.VMEM((tm, D), BF16)],
        compiler_params=_params(("parallel", "arbitrary"), 48),
        name="inproj",
    )(x, g, w)


def _chunk_mean_sq(x, m):
    s = x * x
    hi = s.astype(BF16)
    lo = (s - hi.astype(F32)).astype(BF16)
    return (jnp.dot(hi, m, preferred_element_type=F32)
            + jnp.dot(lo, m, preferred_element_type=F32))


def _lane_mean_sq(x, n):
    return jnp.sum(x * x, axis=-1, keepdims=True) * (1.0 / n)


def _rope(y, cos, sa, sb):
    return y * cos + pltpu.roll(y, 96, 1) * sa + pltpu.roll(y, 32, 1) * sb


def _norm_rope(x, m, gain, cos, sa, sb):
    return _rope(x * lax.rsqrt(_chunk_mean_sq(x, m) + EPS) * gain, cos, sa, sb)


def _rope_tables(pos, dual):
    d = DA_HEAD_DIM
    inv = jnp.power(ROPE_THETA, -jnp.arange(0, d, 2, dtype=F32) / d)
    ang = pos.astype(F32)[:, None] * inv[None, :]
    ang = jnp.concatenate([ang, ang], axis=-1)
    cos, sin = jnp.cos(ang), jnp.sin(ang)
    low = jnp.arange(d) < d // 2
    sa = jnp.where(low, -sin, 0.0)
    sb = jnp.where(low, 0.0, sin)
    if dual:
        return tuple(jnp.concatenate([t, t], axis=-1) for t in (cos, sa, sb))
    return tuple(jnp.concatenate([t, jnp.zeros_like(t)], axis=-1) for t in (cos, sa, sb))


def _avg_matrix(chunk, rows):
    i = jnp.arange(LANES)[:, None]
    j = jnp.arange(LANES)[None, :]
    same = (i // chunk == j // chunk) if rows == LANES else (i < rows) & (j >= 0)
    return jnp.where(same, 1.0 / chunk, 0.0).astype(BF16)


def _da_prep_body(qscale, zq_ref, zkv_ref, m_ref, gq_ref, gk_ref, cos_ref, sa_ref, sb_ref,
                  q_ref, k_ref, v_ref, kb_ref, vb_ref):
    m = m_ref[...]
    cos, sa, sb = cos_ref[...], sa_ref[...], sb_ref[...]
    gq, gk = gq_ref[...], gk_ref[...]
    for c in range(DA_HEADS):
        sl = slice(c * LANES, (c + 1) * LANES)
        q_ref[:, sl] = (_norm_rope(zq_ref[:, sl], m, gq, cos, sa, sb) * qscale).astype(BF16)
    for c in range(DA_KV_HEADS):
        sl = slice(c * LANES, (c + 1) * LANES)
        k = _norm_rope(zkv_ref[:, sl], m, gk, cos, sa, sb)
        k_ref[:, sl] = k
        kb_ref[:, sl] = k.astype(BF16)
    v = zkv_ref[:, DA_KV_HEADS * LANES:]
    v_ref[...] = v
    vb_ref[...] = v.astype(BF16)


def _da_prep(z, gq, gk, tables, tm, n_pos_blocks, qscale):
    T = z.shape[0]
    qw = DA_HEADS * LANES
    kw = DA_KV_HEADS * LANES
    row = lambda i: (i, 0)
    const = lambda i: (0, 0)
    tab = lambda i: (i % n_pos_blocks, 0)
    return pl.pallas_call(
        functools.partial(_da_prep_body, qscale),
        grid=(T // tm,),
        in_specs=[pl.BlockSpec((tm, qw), lambda i: (i, COL_Q // qw)),
                  pl.BlockSpec((tm, 2 * kw), lambda i: (i, COL_KV // (2 * kw))),
                  pl.BlockSpec((LANES, LANES), const),
                  pl.BlockSpec((1, LANES), const),
                  pl.BlockSpec((1, LANES), const),
                  pl.BlockSpec((tm, LANES), tab),
                  pl.BlockSpec((tm, LANES), tab),
                  pl.BlockSpec((tm, LANES), tab)],
        out_specs=[pl.BlockSpec((tm, qw), row),
                   pl.BlockSpec((tm, kw), row),
                   pl.BlockSpec((tm, kw), row),
                   pl.BlockSpec((tm, kw), row),
                   pl.BlockSpec((tm, kw), row)],
        out_shape=[jax.ShapeDtypeStruct((T, qw), BF16),
                   jax.ShapeDtypeStruct((T, kw), F32),
                   jax.ShapeDtypeStruct((T, kw), F32),
                   jax.ShapeDtypeStruct((T, kw), BF16),
                   jax.ShapeDtypeStruct((T, kw), BF16)],
        compiler_params=_params(("parallel",), 40),
        name="da_prep",
    )(z, z, _avg_matrix(DA_HEAD_DIM, LANES), gq, gk, *tables)


def _mla_prep_body(prompt, zqa_ref, zckv_ref, zkpe_ref, gqa_ref, wuq_ref,
                   gqn_ref, gqpe_ref, gckv_ref, gkpe_ref, cos_ref, sa_ref, sb_ref, wx_ref,
                   ckv_ref, kpe_ref, *outs):
    cos, sa, sb = cos_ref[...], sa_ref[...], sb_ref[...]
    gqn, gqpe = gqn_ref[...], gqpe_ref[...]

    xa = zqa_ref[...]
    qa = (xa * lax.rsqrt(jnp.mean(xa * xa, axis=-1, keepdims=True) + EPS) * gqa_ref[...]).astype(BF16)
    xc = zckv_ref[...]
    ckv = xc * lax.rsqrt(jnp.mean(xc * xc, axis=-1, keepdims=True) + EPS) * gckv_ref[...]
    xk = zkpe_ref[...]
    kpe = _rope(xk * lax.rsqrt(_lane_mean_sq(xk, MLA_ROPE) + EPS) * gkpe_ref[...], cos, sa, sb)
    ckv_ref[...] = ckv
    kpe_ref[...] = kpe[:, :MLA_ROPE]

    for h in range(MLA_HEADS):
        qm = jnp.dot(qa, wuq_ref[:, h * MLA_QK:(h + 1) * MLA_QK], preferred_element_type=F32)
        qn = qm[:, :LANES]
        qn = qn * lax.rsqrt(_lane_mean_sq(qn, MLA_NOPE) + EPS) * gqn
        qp = qm[:, LANES:]
        qp = _rope(qp * lax.rsqrt(_lane_mean_sq(qp, MLA_ROPE) + EPS) * gqpe, cos, sa, sb)
        if prompt:
            q_ref = outs[0]
            q_ref[:, h * MLA_QK:h * MLA_QK + LANES] = (qn * (MLA_SCALE * LOG2E)).astype(BF16)
            q_ref[:, h * MLA_QK + LANES:(h + 1) * MLA_QK] = (qp * (MLA_SCALE * LOG2E)).astype(BF16)
        else:
            qlat_ref, qpe_ref = outs
            qlat = jnp.dot(qn.astype(BF16), wx_ref[h], preferred_element_type=F32)
            qlat_ref[:, h * MLA_KV_RANK:(h + 1) * MLA_KV_RANK] = (qlat * MLA_SCALE).astype(BF16)
            qpe_ref[:, h * LANES:(h + 1) * LANES] = (qp * MLA_SCALE).astype(BF16)

    if prompt:
        _, k_ref, v_ref = outs
        kv = jnp.dot(ckv.astype(BF16), wx_ref[...], preferred_element_type=F32)
        kpb = kpe.astype(BF16)
        for h in range(MLA_HEADS):
            k_ref[:, h * MLA_QK:h * MLA_QK + LANES] = kv[:, h * LANES:(h + 1) * LANES].astype(BF16)
            k_ref[:, h * MLA_QK + LANES:(h + 1) * MLA_QK] = kpb
        v_ref[...] = kv[:, MLA_HEADS * MLA_NOPE:].astype(BF16)


def _mla_prep(z, prompt, gqa, wuq, gqn, gqpe, gckv, gkpe, tables, wx, tm, n_pos_blocks):
    T = z.shape[0]
    row = lambda i: (i, 0)
    const = lambda i: (0, 0)
    tab = lambda i: (i % n_pos_blocks, 0)
    once = pl.Buffered(1)
    if prompt:
        wx_spec = pl.BlockSpec(wx.shape, const, pipeline_mode=once)
        extra_specs = [pl.BlockSpec((tm, MLA_HEADS * MLA_QK), row),
                       pl.BlockSpec((tm, MLA_HEADS * MLA_QK), row),
                       pl.BlockSpec((tm, MLA_WIDTH), row)]
        extra_shapes = [jax.ShapeDtypeStruct((T, MLA_HEADS * MLA_QK), BF16),
                        jax.ShapeDtypeStruct((T, MLA_HEADS * MLA_QK), BF16),
                        jax.ShapeDtypeStruct((T, MLA_WIDTH), BF16)]
    else:
        wx_spec = pl.BlockSpec(wx.shape, lambda i: (0, 0, 0), pipeline_mode=once)
        extra_specs = [pl.BlockSpec((tm, MLA_HEADS * MLA_KV_RANK), row),
                       pl.BlockSpec((tm, MLA_HEADS * LANES), row)]
        extra_shapes = [jax.ShapeDtypeStruct((T, MLA_HEADS * MLA_KV_RANK), BF16),
                        jax.ShapeDtypeStruct((T, MLA_HEADS * LANES), BF16)]
    return pl.pallas_call(
        functools.partial(_mla_prep_body, prompt),
        grid=(T // tm,),
        in_specs=[pl.BlockSpec((tm, MLA_Q_RANK), lambda i: (i, COL_QA // MLA_Q_RANK)),
                  pl.BlockSpec((tm, MLA_KV_RANK), lambda i: (i, COL_CKV // MLA_KV_RANK)),
                  pl.BlockSpec((tm, LANES), lambda i: (i, COL_KPE // LANES)),
                  pl.BlockSpec((1, MLA_Q_RANK), const),
                  pl.BlockSpec(wuq.shape, const, pipeline_mode=once),
                  pl.BlockSpec((1, LANES), const),
                  pl.BlockSpec((1, LANES), const),
                  pl.BlockSpec((1, MLA_KV_RANK), const),
                  pl.BlockSpec((1, LANES), const),
                  pl.BlockSpec((tm, LANES), tab),
                  pl.BlockSpec((tm, LANES), tab),
                  pl.BlockSpec((tm, LANES), tab),
                  wx_spec],
        out_specs=[pl.BlockSpec((tm, MLA_KV_RANK), row),
                   pl.BlockSpec((tm, MLA_ROPE), row)] + extra_specs,
        out_shape=[jax.ShapeDtypeStruct((T, MLA_KV_RANK), F32),
                   jax.ShapeDtypeStruct((T, MLA_ROPE), F32)] + extra_shapes,
        compiler_params=_params(("parallel",), 56),
        name="mla_prep_prompt" if prompt else "mla_prep_sample",
    )(z, z, z, gqa, wuq, gqn, gqpe, gckv, gkpe, *tables, wx)


def _da_lambda(lv):
    e1 = jnp.exp(jnp.sum(lv[0:1] * lv[1:2], axis=-1, keepdims=True))
    e2 = jnp.exp(jnp.sum(lv[2:3] * lv[3:4], axis=-1, keepdims=True))
    return e1 - e2 + LAM_INIT


SUM_ROWS = 8


def _with_sum_rows(vt):
    return jnp.concatenate([vt, jnp.ones(vt.shape[:-2] + (SUM_ROWS, vt.shape[-1]), vt.dtype)], axis=-2)


def _softmax_probs_t(s, m_ref, idx):
    m_all = m_ref[idx]
    ms, ps, corrs = [], [], []
    for j in range(s.shape[1] // LANES):
        cs = slice(j * LANES, (j + 1) * LANES)
        sj = s[:, cs]
        m_prev = m_all[:, cs]
        m_new = jnp.maximum(m_prev, jnp.max(sj, axis=0, keepdims=True))
        corr = jnp.exp2(m_prev - m_new)
        ms.append(m_new)
        ps.append(jnp.exp2(sj - m_new).astype(BF16))
        corrs.append(corr)
    m_ref[idx] = jnp.concatenate(ms, axis=1)
    return jnp.concatenate(ps, axis=1), jnp.concatenate(corrs, axis=1)


def _accumulate_t(acc_ref, idx, v_t, p, corr):
    acc_ref[idx] = acc_ref[idx] * corr + jnp.dot(v_t, p, preferred_element_type=F32)


def _da_attn_body(qt_ref, k_ref, vt_ref, g_ref, lv_ref, gs_ref, o_ref, qbd_ref, m_ref, acc_ref, *, t):
    qi = pl.program_id(2)
    cols = 2 * DA_GROUP * t
    low = lax.broadcasted_iota(jnp.int32, (LANES, 1), 0) < DA_HEAD_DIM
    for r in range(DA_GROUP):
        q = qt_ref[r * LANES:(r + 1) * LANES, :]
        zero = jnp.zeros_like(q)
        qbd_ref[:, (2 * r) * t:(2 * r + 1) * t] = jnp.where(low, q, zero)
        qbd_ref[:, (2 * r + 1) * t:(2 * r + 2) * t] = jnp.where(low, zero, q)
    m_ref[...] = jnp.full(m_ref.shape, NEG_INF, F32)
    acc_ref[...] = jnp.zeros(acc_ref.shape, F32)

    def step(ki, masked):
        k = k_ref[pl.ds(pl.multiple_of(ki * t, t), t), :]
        s = jnp.dot(k, qbd_ref[...], preferred_element_type=F32)
        if masked:
            key = lax.broadcasted_iota(jnp.int32, (t, cols), 0)
            qry = lax.broadcasted_iota(jnp.int32, (t, cols), 1) % t
            s = jnp.where(key <= qry, s, NEG_INF)
        p, corr = _softmax_probs_t(s, m_ref, 0)
        _accumulate_t(acc_ref, 0, vt_ref[0, ki], p, corr)

    def loop_body(ki, carry):
        step(ki, False)
        return carry

    lax.fori_loop(0, qi, loop_body, 0)
    step(qi, True)

    lam = _da_lambda(lv_ref[...])
    gs = gs_ref[...] * (1.0 - LAM_INIT)
    inv_l = 1.0 / acc_ref[0, DA_V_DIM:DA_V_DIM + 1, :]
    for r in range(DA_GROUP):
        a1 = slice((2 * r) * t, (2 * r + 1) * t)
        a2 = slice((2 * r + 1) * t, (2 * r + 2) * t)
        od = (acc_ref[0, :DA_V_DIM, a1] * inv_l[:, a1]
              - lam * (acc_ref[0, :DA_V_DIM, a2] * inv_l[:, a2]))
        y = (od * lax.rsqrt(jnp.mean(od * od, axis=0, keepdims=True) + EPS)).T
        sl = slice(r * LANES, (r + 1) * LANES)
        o_ref[:, sl] = (y * gs * _silu(g_ref[:, sl])).astype(BF16)


def _da_attn(qt, kb, vt, z, lv, gs, B, S, t):
    T = kb.shape[0]
    nq = S // t
    gw = DA_GROUP * LANES
    cols = 2 * DA_GROUP * t
    return pl.pallas_call(
        functools.partial(_da_attn_body, t=t),
        grid=(B, DA_KV_HEADS, nq),
        in_specs=[pl.BlockSpec((gw, t), lambda b, g, i: (g, b * nq + i)),
                  pl.BlockSpec((S, LANES), lambda b, g, i: (b, g)),
                  pl.BlockSpec((1, nq, DA_V_DIM + SUM_ROWS, t), lambda b, g, i: (g, b, 0, 0)),
                  pl.BlockSpec((t, gw), lambda b, g, i: (b * nq + i, COL_GA // gw + g)),
                  pl.BlockSpec((4, DA_HEAD_DIM), lambda b, g, i: (0, 0)),
                  pl.BlockSpec((1, LANES), lambda b, g, i: (0, 0))],
        out_specs=pl.BlockSpec((t, gw), lambda b, g, i: (b * nq + i, g)),
        out_shape=jax.ShapeDtypeStruct((T, DA_WIDTH), BF16),
        scratch_shapes=[pltpu.VMEM((LANES, cols), BF16),
                        pltpu.VMEM((1, 1, cols), F32),
                        pltpu.VMEM((1, DA_V_DIM + SUM_ROWS, cols), F32)],
        compiler_params=_params(("parallel", "parallel", "arbitrary"), 48),
        name="da_attn",
    )(qt, kb, vt, z, lv, gs)


def _mla_attn_body(qt_ref, k_ref, vt_ref, g_ref, o_ref, m_ref, acc_ref, *, t, hb):
    qi = pl.program_id(2)
    m_ref[...] = jnp.full(m_ref.shape, NEG_INF, F32)
    acc_ref[...] = jnp.zeros(acc_ref.shape, F32)

    def step(ki, masked):
        row0 = pl.multiple_of(ki * t, t)
        scores = []
        for h in range(hb):
            k = k_ref[pl.ds(row0, t), h * MLA_QK:(h + 1) * MLA_QK]
            s = jnp.dot(k, qt_ref[h * MLA_QK:(h + 1) * MLA_QK, :], preferred_element_type=F32)
            if masked:
                key = lax.broadcasted_iota(jnp.int32, (t, t), 0)
                qry = lax.broadcasted_iota(jnp.int32, (t, t), 1)
                s = jnp.where(key <= qry, s, NEG_INF)
            scores.append(s)
        probs = [_softmax_probs_t(scores[h], m_ref, h) for h in range(hb)]
        for h in range(hb):
            _accumulate_t(acc_ref, h, vt_ref[h, ki], *probs[h])

    def loop_body(ki, carry):
        step(ki, False)
        return carry

    lax.fori_loop(0, qi, loop_body, 0)
    step(qi, True)

    for h in range(hb):
        sl = slice(h * MLA_V, (h + 1) * MLA_V)
        o = (acc_ref[h, :MLA_V, :] * (1.0 / acc_ref[h, MLA_V:MLA_V + 1, :])).T
        o_ref[:, sl] = (o * _silu(g_ref[:, sl])).astype(BF16)


def _mla_attn(qt, k, vt, z, B, S, t, hb):
    T = k.shape[0]
    nq = S // t
    return pl.pallas_call(
        functools.partial(_mla_attn_body, t=t, hb=hb),
        grid=(B, MLA_HEADS // hb, nq),
        in_specs=[pl.BlockSpec((hb * MLA_QK, t), lambda b, h, i: (h, b * nq + i)),
                  pl.BlockSpec((S, hb * MLA_QK), lambda b, h, i: (b, h)),
                  pl.BlockSpec((hb, nq, MLA_V + SUM_ROWS, t), lambda b, h, i: (h, b, 0, 0)),
                  pl.BlockSpec((t, hb * MLA_V), lambda b, h, i: (b * nq + i, COL_GB // (hb * MLA_V) + h))],
        out_specs=pl.BlockSpec((t, hb * MLA_V), lambda b, h, i: (b * nq + i, h)),
        out_shape=jax.ShapeDtypeStruct((T, MLA_WIDTH), BF16),
        scratch_shapes=[pltpu.VMEM((hb, 1, t), F32),
                        pltpu.VMEM((hb, MLA_V + SUM_ROWS, t), F32)],
        compiler_params=_params(("parallel", "parallel", "arbitrary"), 48),
        name="mla_attn",
    )(qt, k, vt, z)


def _online_update(m_ref, l_ref, acc_ref, s, pv_fn):
    m_prev = m_ref[...]
    m_new = jnp.maximum(m_prev, jnp.max(s, axis=-1, keepdims=True))
    corr = jnp.exp(m_prev - m_new)
    p = jnp.exp(s - m_new)
    l_ref[...] = l_ref[...] * corr + jnp.sum(p, axis=-1, keepdims=True)
    acc_ref[...] = acc_ref[...] * corr + pv_fn(p)
    m_ref[...] = m_new


def _decode_body(pt_ref, qda_ref, kn_ref, vn_ref, qlat_ref, qpe_ref, ckvn_ref, kpen_ref, lv_ref,
                 ck_hbm, kp_hbm, dk_hbm, dv_hbm, oda_ref, olat_ref,
                 ck_buf, kp_buf, dk_buf, dv_buf, sem,
                 m_ml, l_ml, acc_ml, m_da, l_da, acc_da, *, P, NC, page):
    n = pl.program_id(0)
    c = n % NC
    slot = n % 2
    nt = (((1,), (1,)), ((), ()))

    def copies(n_, slot_, real):
        b_ = n_ // NC
        c_ = n_ % NC
        out = []
        for j in range(P):
            pg = pt_ref[b_, c_ * P + j] if real else 0
            out.append(pltpu.make_async_copy(
                ck_hbm.at[pg], ck_buf.at[slot_, pl.ds(j * page, page), :], sem.at[0, slot_]))
            out.append(pltpu.make_async_copy(
                kp_hbm.at[pg], kp_buf.at[slot_, :, pl.ds(j * page, page)], sem.at[1, slot_]))
            out.append(pltpu.make_async_copy(
                dk_hbm.at[pg], dk_buf.at[slot_, pl.ds(j * 2 * page, 2 * page), :], sem.at[2, slot_]))
            out.append(pltpu.make_async_copy(
                dv_hbm.at[pg], dv_buf.at[slot_, pl.ds(j * 2 * page, 2 * page), :], sem.at[3, slot_]))
        return out

    @pl.when(n == 0)
    def _():
        for cp in copies(0, 0, True):
            cp.start()

    @pl.when(n + 1 < pl.num_programs(0))
    def _():
        for cp in copies(n + 1, 1 - slot, True):
            cp.start()

    for cp in copies(n, slot, False):
        cp.wait()

    @pl.when(c == 0)
    def _():
        for m_ref, l_ref, acc_ref in ((m_ml, l_ml, acc_ml), (m_da, l_da, acc_da)):
            m_ref[...] = jnp.full(m_ref.shape, NEG_INF, F32)
            l_ref[...] = jnp.zeros(l_ref.shape, F32)
            acc_ref[...] = jnp.zeros(acc_ref.shape, F32)

    ck = ck_buf[slot].astype(BF16)
    kp = kp_buf[slot].astype(BF16)
    ql = qlat_ref[0]
    qp = qpe_ref[0][:, :MLA_ROPE]
    s = (lax.dot_general(ql, ck, nt, preferred_element_type=F32)
         + jnp.dot(qp, kp, preferred_element_type=F32))
    _online_update(m_ml, l_ml, acc_ml, s,
                   lambda p: jnp.dot(p.astype(BF16), ck, preferred_element_type=F32))

    low = lax.broadcasted_iota(jnp.int32, (1, LANES), 1) < DA_HEAD_DIM
    qda = qda_ref[0].astype(F32)
    parts = []
    for g in range(DA_KV_HEADS):
        q = qda[g * DA_GROUP:(g + 1) * DA_GROUP, :]
        zero = jnp.zeros_like(q)
        parts += [jnp.where(low, q, zero), jnp.where(low, zero, q)]
    q32 = jnp.concatenate(parts, axis=0)
    dk = dk_buf[slot].astype(BF16)
    dv = dv_buf[slot].astype(BF16)
    s = lax.dot_general(q32.astype(BF16), dk, nt, preferred_element_type=F32)
    col_head = lax.broadcasted_iota(jnp.int32, s.shape, 1) % DA_KV_HEADS
    row_head = lax.broadcasted_iota(jnp.int32, s.shape, 0) // (2 * DA_GROUP)
    s = jnp.where(col_head == row_head, s, NEG_INF)
    _online_update(m_da, l_da, acc_da, s,
                   lambda p: jnp.dot(p.astype(BF16), dv, preferred_element_type=F32))

    @pl.when(c == NC - 1)
    def _():
        ckvn = ckvn_ref[0]
        s = (jnp.sum(ql.astype(F32) * ckvn, axis=-1, keepdims=True)
             + jnp.sum(qp.astype(F32) * kpen_ref[0], axis=-1, keepdims=True))
        _online_update(m_ml, l_ml, acc_ml, s, lambda p: p * ckvn)
        olat_ref[0] = acc_ml[...] / l_ml[...]

        per_head = lambda x: jnp.concatenate(
            [jnp.broadcast_to(x[:, g * LANES:(g + 1) * LANES], (2 * DA_GROUP, LANES))
             for g in range(DA_KV_HEADS)], axis=0)
        kn = per_head(kn_ref[0])
        vn = per_head(vn_ref[0])
        s = jnp.sum(q32 * kn, axis=-1, keepdims=True)
        _online_update(m_da, l_da, acc_da, s, lambda p: p * vn)
        o = acc_da[...] / l_da[...]
        lam = _da_lambda(lv_ref[...])
        for g in range(DA_KV_HEADS):
            r0 = g * 2 * DA_GROUP
            oda_ref[0, g * DA_GROUP:(g + 1) * DA_GROUP, :] = (
                o[r0:r0 + DA_GROUP] - lam * o[r0 + DA_GROUP:r0 + 2 * DA_GROUP])


def _decode(page_table, qda, kn, vn, qlat, qpe, ckvn, kpen, lv, ck, kp, dk, dv, P):
    Bd, n_pages = page_table.shape
    page = ck.shape[1]
    NC = n_pages // P
    rows_da = 2 * DA_KV_HEADS * DA_GROUP
    blk = lambda n, pt: (n // NC, 0, 0)
    any_spec = pl.BlockSpec(memory_space=pl.ANY)
    grid_spec = pltpu.PrefetchScalarGridSpec(
        num_scalar_prefetch=1,
        grid=(Bd * NC,),
        in_specs=[pl.BlockSpec((1, DA_HEADS, LANES), blk),
                  pl.BlockSpec((1, 1, DA_KV_HEADS * LANES), blk),
                  pl.BlockSpec((1, 1, DA_KV_HEADS * LANES), blk),
                  pl.BlockSpec((1, MLA_HEADS, MLA_KV_RANK), blk),
                  pl.BlockSpec((1, MLA_HEADS, LANES), blk),
                  pl.BlockSpec((1, 1, MLA_KV_RANK), blk),
                  pl.BlockSpec((1, 1, MLA_ROPE), blk),
                  pl.BlockSpec((4, DA_HEAD_DIM), lambda n, pt: (0, 0)),
                  any_spec, any_spec, any_spec, any_spec],
        out_specs=[pl.BlockSpec((1, DA_HEADS, LANES), blk),
                   pl.BlockSpec((1, MLA_HEADS, MLA_KV_RANK), blk)],
        scratch_shapes=[pltpu.VMEM((2, P * page, MLA_KV_RANK), F32),
                        pltpu.VMEM((2, MLA_ROPE, P * page), F32),
                        pltpu.VMEM((2, P * page * DA_KV_HEADS, LANES), F32),
                        pltpu.VMEM((2, P * page * DA_KV_HEADS, LANES), F32),
                        pltpu.SemaphoreType.DMA((4, 2)),
                        pltpu.VMEM((MLA_HEADS, 1), F32),
                        pltpu.VMEM((MLA_HEADS, 1), F32),
                        pltpu.VMEM((MLA_HEADS, MLA_KV_RANK), F32),
                        pltpu.VMEM((rows_da, 1), F32),
                        pltpu.VMEM((rows_da, 1), F32),
                        pltpu.VMEM((rows_da, LANES), F32)])
    return pl.pallas_call(
        functools.partial(_decode_body, P=P, NC=NC, page=page),
        grid_spec=grid_spec,
        out_shape=[jax.ShapeDtypeStruct((Bd, DA_HEADS, LANES), F32),
                   jax.ShapeDtypeStruct((Bd, MLA_HEADS, MLA_KV_RANK), F32)],
        compiler_params=_params(("arbitrary",), 56),
        name="decode",
    )(page_table, qda, kn, vn, qlat, qpe, ckvn, kpen, lv, ck, kp, dk, dv)


def _sample_post_body(oda_ref, olat_ref, ga_ref, gb_ref, gs_ref, wuv_ref, oa_ref, ob_ref):
    gs = gs_ref[...]
    for h in range(DA_HEADS):
        sl = slice(h * LANES, (h + 1) * LANES)
        od = oda_ref[:, sl]
        y = od * lax.rsqrt(jnp.mean(od * od, axis=-1, keepdims=True) + EPS) * gs * (1.0 - LAM_INIT)
        oa_ref[:, sl] = (y * _silu(ga_ref[:, sl])).astype(BF16)
    for h in range(MLA_HEADS):
        sl = slice(h * MLA_V, (h + 1) * MLA_V)
        ol = olat_ref[:, h * MLA_KV_RANK:(h + 1) * MLA_KV_RANK].astype(BF16)
        ob = jnp.dot(ol, wuv_ref[h], preferred_element_type=F32)
        ob_ref[:, sl] = (ob * _silu(gb_ref[:, sl])).astype(BF16)


def _sample_post(oda, olat, z, gs, wuv):
    Bd = oda.shape[0]
    const = lambda i: (0, 0)
    return pl.pallas_call(
        _sample_post_body,
        grid=(1,),
        in_specs=[pl.BlockSpec((Bd, DA_WIDTH), const),
                  pl.BlockSpec((Bd, MLA_HEADS * MLA_KV_RANK), const),
                  pl.BlockSpec((Bd, DA_WIDTH), lambda i: (0, COL_GA // DA_WIDTH)),
                  pl.BlockSpec((Bd, MLA_WIDTH), lambda i: (0, COL_GB // MLA_WIDTH)),
                  pl.BlockSpec((1, LANES), const),
                  pl.BlockSpec(wuv.shape, lambda i: (0, 0, 0))],
        out_specs=[pl.BlockSpec((Bd, DA_WIDTH), const),
                   pl.BlockSpec((Bd, MLA_WIDTH), const)],
        out_shape=[jax.ShapeDtypeStruct((Bd, DA_WIDTH), BF16),
                   jax.ShapeDtypeStruct((Bd, MLA_WIDTH), BF16)],
        compiler_params=_params(("arbitrary",), 40),
        name="sample_post",
    )(oda, olat, z, z, gs, wuv)


def _outproj_body(oa_ref, ob_ref, w_ref, x_ref, y_ref):
    ka = oa_ref.shape[1]
    y_ref[...] = (x_ref[...]
                  + jnp.dot(oa_ref[...], w_ref[:ka, :], preferred_element_type=F32)
                  + jnp.dot(ob_ref[...], w_ref[ka:, :], preferred_element_type=F32))


def _outproj(oa, ob, w, x, tm, tn):
    T, D = x.shape
    return pl.pallas_call(
        _outproj_body,
        grid=(T // tm, D // tn),
        in_specs=[pl.BlockSpec((tm, oa.shape[1]), lambda i, j: (i, 0)),
                  pl.BlockSpec((tm, ob.shape[1]), lambda i, j: (i, 0)),
                  pl.BlockSpec((w.shape[0], tn), lambda i, j: (0, j)),
                  pl.BlockSpec((tm, tn), lambda i, j: (i, j))],
        out_specs=pl.BlockSpec((tm, tn), lambda i, j: (i, j)),
        out_shape=jax.ShapeDtypeStruct((T, D), F32),
        compiler_params=_params(("parallel", "arbitrary"), 48),
        name="outproj",
    )(oa, ob, w, x)


def _row_tile(T, cap):
    t = min(T, cap)
    while T % t:
        t //= 2
    return t


def kernel(x_prompt, x_sample, cache_diff_k, cache_diff_v, cache_mla_ckv, cache_mla_kpe, page_table,
           g_pre, w_in, da_q_norm, da_k_norm, da_lambda_q1, da_lambda_k1, da_lambda_q2, da_lambda_k2,
           da_subln, mla_q_a_norm, w_uq, mla_qn_norm, mla_qpe_norm, mla_kv_a_norm, mla_kpe_norm,
           w_uk, w_uv, w_out):
    B, S, D = x_prompt.shape
    Bd, Td, _ = x_sample.shape
    depth, n_phys, page = cache_mla_ckv.shape[:3]
    n_pages = page_table.shape[1]
    assert depth == 1 and Td == 1
    assert w_uq.shape[1:] == (MLA_Q_RANK, MLA_HEADS, MLA_NOPE + MLA_ROPE)
    assert w_uk.shape[1:] == (MLA_KV_RANK, MLA_HEADS, MLA_NOPE)
    assert w_uv.shape[1:] == (MLA_KV_RANK, MLA_HEADS, MLA_V)
    assert cache_diff_k.shape[3:] == (DA_KV_HEADS, 2 * DA_HEAD_DIM)
    assert w_in.shape[2] == COL_KPE + MLA_ROPE

    w = w_in[0].T
    sizes = (DA_HEADS * 2 * DA_HEAD_DIM, DA_KV_HEADS * 2 * DA_HEAD_DIM, DA_KV_HEADS * DA_V_DIM, DA_WIDTH,
             MLA_Q_RANK, MLA_KV_RANK, MLA_ROPE, MLA_WIDTH)
    offs = [0]
    for s_ in sizes:
        offs.append(offs[-1] + s_)
    sec = [w[offs[i]:offs[i + 1]] for i in range(len(sizes))]
    zq, zk, zv, ga, zqa, zckv, zkpe, gb = sec
    w_perm = jnp.concatenate(
        [s_.astype(BF16) for s_ in (zq, ga, gb, zqa, zckv, zk, zv, zkpe)]
        + [jnp.zeros((N_PAD - COL_KPE - MLA_ROPE, D), BF16)], axis=0)
    wuq = w_uq[0]
    wuq_pad = jnp.concatenate(
        [wuq, jnp.zeros((MLA_Q_RANK, MLA_HEADS, MLA_QK - MLA_NOPE - MLA_ROPE), wuq.dtype)], axis=-1)
    wuq_pad = wuq_pad.reshape(MLA_Q_RANK, MLA_HEADS * MLA_QK).astype(BF16)
    wkv = jnp.concatenate([w_uk[0].reshape(MLA_KV_RANK, -1), w_uv[0].reshape(MLA_KV_RANK, -1)],
                          axis=1).astype(BF16)
    wuk_t = jnp.transpose(w_uk[0], (1, 2, 0)).astype(BF16)
    wuv_h = jnp.transpose(w_uv[0], (1, 0, 2)).astype(BF16)
    w_o = w_out[0].astype(BF16)

    tile2 = lambda g: jnp.concatenate([g, g], axis=-1)
    pad2 = lambda g: jnp.concatenate([g, jnp.zeros_like(g)], axis=-1)
    gq, gk = tile2(da_q_norm), tile2(da_k_norm)
    gqpe, gkpe = pad2(mla_qpe_norm), pad2(mla_kpe_norm)
    lv = jnp.concatenate([da_lambda_q1, da_lambda_k1, da_lambda_q2, da_lambda_k2], axis=0)

    Tp = B * S
    xp = x_prompt.reshape(Tp, D)
    tm = _row_tile(S, 512)
    pos_p = jnp.arange(S)
    z = _inproj(xp, g_pre, w_perm, tm)
    q_da, k_da, v_da, kb_da, vb_da = _da_prep(z, gq, gk, _rope_tables(pos_p, True), tm, S // tm,
                                              DA_SCALE * LOG2E)
    tm_ml = _row_tile(S, 512)
    ckv_p, kpe_p, q_ml, k_ml, v_ml = _mla_prep(
        z, True, mla_q_a_norm, wuq_pad, mla_qn_norm, gqpe, mla_kv_a_norm, gkpe,
        _rope_tables(pos_p, False), wkv, tm_ml, S // tm_ml)
    t_da = _row_tile(S, 256)
    t_ml = _row_tile(S, 512)
    key_tiles_t = lambda v, t, heads, dim: _with_sum_rows(jnp.transpose(
        v.reshape(Tp // t, t, heads, dim), (2, 0, 3, 1)))
    o_a = _da_attn(q_da.T, kb_da, key_tiles_t(vb_da, t_da, DA_KV_HEADS, DA_V_DIM), z, lv, da_subln,
                   B, S, t_da)
    o_b = _mla_attn(q_ml.T, k_ml, key_tiles_t(v_ml, t_ml, MLA_HEADS, MLA_V), z, B, S, t_ml, 4)
    y_p = _outproj(o_a, o_b, w_o, xp, _row_tile(Tp, 1024), 512).reshape(B, S, D)

    xs = x_sample.reshape(Bd, D)
    pos_s = jnp.full((Bd,), n_pages * page, jnp.int32)
    zs = _inproj(xs, g_pre, w_perm, Bd)
    qs_da, ks_da, vs_da, _, _ = _da_prep(zs, gq, gk, _rope_tables(pos_s, True), Bd, 1, DA_SCALE)
    ckv_s, kpe_s, qlat, qpe = _mla_prep(
        zs, False, mla_q_a_norm, wuq_pad, mla_qn_norm, gqpe, mla_kv_a_norm, gkpe,
        _rope_tables(pos_s, False), wuk_t, Bd, 1)
    kvw = DA_KV_HEADS * LANES
    oda, olat = _decode(
        page_table,
        qs_da.reshape(Bd, DA_HEADS, LANES), ks_da.reshape(Bd, 1, kvw), vs_da.reshape(Bd, 1, kvw),
        qlat.reshape(Bd, MLA_HEADS, MLA_KV_RANK), qpe.reshape(Bd, MLA_HEADS, LANES),
        ckv_s.reshape(Bd, 1, MLA_KV_RANK), kpe_s.reshape(Bd, 1, MLA_ROPE), lv,
        cache_mla_ckv.reshape(n_phys, page, MLA_KV_RANK),
        jnp.swapaxes(cache_mla_kpe.reshape(n_phys, page, MLA_ROPE), 1, 2),
        cache_diff_k.reshape(n_phys, page * DA_KV_HEADS, LANES),
        cache_diff_v.reshape(n_phys, page * DA_KV_HEADS, LANES),
        _row_tile(n_pages, 32))
    oa_s, ob_s = _sample_post(oda.reshape(Bd, DA_WIDTH), olat.reshape(Bd, MLA_HEADS * MLA_KV_RANK),
                              zs, da_subln, wuv_h)
    y_s = _outproj(oa_s, ob_s, w_o, xs, Bd, 512).reshape(Bd, 1, D)

    return (y_p, y_s,
            k_da.reshape(1, B, S, DA_KV_HEADS, 2 * DA_HEAD_DIM), v_da.reshape(1, B, S, DA_KV_HEADS, DA_V_DIM),
            ckv_p.reshape(1, B, S, MLA_KV_RANK), kpe_p.reshape(1, B, S, MLA_ROPE),
            ks_da.reshape(1, Bd, 1, DA_KV_HEADS, 2 * DA_HEAD_DIM), vs_da.reshape(1, Bd, 1, DA_KV_HEADS, DA_V_DIM),
            ckv_s.reshape(1, Bd, 1, MLA_KV_RANK), kpe_s.reshape(1, Bd, 1, MLA_ROPE))
```

```python
import functools
import math

import jax
import jax.numpy as jnp
from jax import lax
from jax.experimental import pallas as pl
from jax.experimental.pallas import tpu as pltpu

F32 = jnp.float32
BF16 = jnp.bfloat16

EPS = 1e-6
ROPE_THETA = 10000.0
NEG_INF = -1e30
LANES = 128
MIB = 1024 * 1024

DA_HEAD_DIM = 64
DA_V_DIM = 128
DA_HEADS = 16
DA_KV_HEADS = 2
DA_GROUP = DA_HEADS // DA_KV_HEADS
DA_WIDTH = DA_HEADS * DA_V_DIM
DA_SCALE = DA_HEAD_DIM ** -0.5
MLA_HEADS = 16
MLA_NOPE = 128
MLA_ROPE = 64
MLA_V = 128
MLA_KV_RANK = 512
MLA_Q_RANK = 1024
MLA_WIDTH = MLA_HEADS * MLA_V
MLA_SCALE = (MLA_NOPE + MLA_ROPE) ** -0.5
MLA_QK = 2 * LANES
LAM_INIT = 0.8 - 0.6 * math.exp(-0.3 * 0)
LOG2E = math.log2(math.e)

COL_Q = 0
COL_GA = COL_Q + DA_HEADS * 2 * DA_HEAD_DIM
COL_GB = COL_GA + DA_WIDTH
COL_QA = COL_GB + MLA_WIDTH
COL_CKV = COL_QA + MLA_Q_RANK
COL_KV = COL_CKV + MLA_KV_RANK
COL_KPE = COL_KV + 2 * DA_KV_HEADS * DA_V_DIM
INPROJ_TN = 768
N_PAD = COL_KPE + 256
assert N_PAD % INPROJ_TN == 0


def _params(sem, vmem_mib):
    return pltpu.CompilerParams(dimension_semantics=sem, vmem_limit_bytes=vmem_mib * MIB)


def _silu(g):
    return g / (1.0 + jnp.exp(-g))


def _inproj_body(x_ref, g_ref, w_ref, z_ref, h_ref):
    @pl.when(pl.program_id(1) == 0)
    def _():
        x = x_ref[...]
        ms = jnp.mean(x * x, axis=-1, keepdims=True)
        h_ref[...] = (x * lax.rsqrt(ms + EPS) * g_ref[...]).astype(BF16)

    z_ref[...] = lax.dot_general(h_ref[...], w_ref[...], (((1,), (1,)), ((), ())),
                                 preferred_element_type=F32)


def _inproj(x, g, w, tm):
    T, D = x.shape
    N = w.shape[0]
    tn = INPROJ_TN
    return pl.pallas_call(
        _inproj_body,
        grid=(T // tm, N // tn),
        in_specs=[pl.BlockSpec((tm, D), lambda i, j: (i, 0)),
                  pl.BlockSpec((1, D), lambda i, j: (0, 0)),
                  pl.BlockSpec((tn, D), lambda i, j: (j, 0))],
        out_specs=pl.BlockSpec((tm, tn), lambda i, j: (i, j)),
        out_shape=jax.ShapeDtypeStruct((T, N), F32),
        scratch_shapes=[pltpu.VMEM((tm, D), BF16)],
        compiler_params=_params(("parallel", "arbitrary"), 48),
        name="inproj",
    )(x, g, w)


def _chunk_mean_sq(x, m):
    s = x * x
    hi = s.astype(BF16)
    lo = (s - hi.astype(F32)).astype(BF16)
    return (jnp.dot(hi, m, preferred_element_type=F32)
            + jnp.dot(lo, m, preferred_element_type=F32))


def _lane_mean_sq(x, n):
    return jnp.sum(x * x, axis=-1, keepdims=True) * (1.0 / n)


def _rope(y, cos, sa, sb):
    return y * cos + pltpu.roll(y, 96, 1) * sa + pltpu.roll(y, 32, 1) * sb


def _norm_rope(x, m, gain, cos, sa, sb):
    return _rope(x * lax.rsqrt(_chunk_mean_sq(x, m) + EPS) * gain, cos, sa, sb)


def _rope_tables(pos, dual):
    d = DA_HEAD_DIM
    inv = jnp.power(ROPE_THETA, -jnp.arange(0, d, 2, dtype=F32) / d)
    ang = pos.astype(F32)[:, None] * inv[None, :]
    ang = jnp.concatenate([ang, ang], axis=-1)
    cos, sin = jnp.cos(ang), jnp.sin(ang)
    low = jnp.arange(d) < d // 2
    sa = jnp.where(low, -sin, 0.0)
    sb = jnp.where(low, 0.0, sin)
    if dual:
        return tuple(jnp.concatenate([t, t], axis=-1) for t in (cos, sa, sb))
    return tuple(jnp.concatenate([t, jnp.zeros_like(t)], axis=-1) for t in (cos, sa, sb))


def _avg_matrix(chunk, rows):
    i = jnp.arange(LANES)[:, None]
    j = jnp.arange(LANES)[None, :]
    same = (i // chunk == j // chunk) if rows == LANES else (i < rows) & (j >= 0)
    return jnp.where(same, 1.0 / chunk, 0.0).astype(BF16)


def _da_prep_body(qscale, zq_ref, zkv_ref, m_ref, gq_ref, gk_ref, cos_ref, sa_ref, sb_ref,
                  q_ref, k_ref, v_ref, kb_ref, vb_ref):
    m = m_ref[...]
    cos, sa, sb = cos_ref[...], sa_ref[...], sb_ref[...]
    gq, gk = gq_ref[...], gk_ref[...]
    for c in range(DA_HEADS):
        sl = slice(c * LANES, (c + 1) * LANES)
        q_ref[:, sl] = (_norm_rope(zq_ref[:, sl], m, gq, cos, sa, sb) * qscale).astype(BF16)
    for c in range(DA_KV_HEADS):
        sl = slice(c * LANES, (c + 1) * LANES)
        k = _norm_rope(zkv_ref[:, sl], m, gk, cos, sa, sb)
        k_ref[:, sl] = k
        kb_ref[:, sl] = k.astype(BF16)
    v = zkv_ref[:, DA_KV_HEADS * LANES:]
    v_ref[...] = v
    vb_ref[...] = v.astype(BF16)


def _da_prep(z, gq, gk, tables, tm, n_pos_blocks, qscale):
    T = z.shape[0]
    qw = DA_HEADS * LANES
    kw = DA_KV_HEADS * LANES
    row = lambda i: (i, 0)
    const = lambda i: (0, 0)
    tab = lambda i: (i % n_pos_blocks, 0)
    return pl.pallas_call(
        functools.partial(_da_prep_body, qscale),
        grid=(T // tm,),
        in_specs=[pl.BlockSpec((tm, qw), lambda i: (i, COL_Q // qw)),
                  pl.BlockSpec((tm, 2 * kw), lambda i: (i, COL_KV // (2 * kw))),
                  pl.BlockSpec((LANES, LANES), const),
                  pl.BlockSpec((1, LANES), const),
                  pl.BlockSpec((1, LANES), const),
                  pl.BlockSpec((tm, LANES), tab),
                  pl.BlockSpec((tm, LANES), tab),
                  pl.BlockSpec((tm, LANES), tab)],
        out_specs=[pl.BlockSpec((tm, qw), row),
                   pl.BlockSpec((tm, kw), row),
                   pl.BlockSpec((tm, kw), row),
                   pl.BlockSpec((tm, kw), row),
                   pl.BlockSpec((tm, kw), row)],
        out_shape=[jax.ShapeDtypeStruct((T, qw), BF16),
                   jax.ShapeDtypeStruct((T, kw), F32),
                   jax.ShapeDtypeStruct((T, kw), F32),
                   jax.ShapeDtypeStruct((T, kw), BF16),
                   jax.ShapeDtypeStruct((T, kw), BF16)],
        compiler_params=_params(("parallel",), 40),
        name="da_prep",
    )(z, z, _avg_matrix(DA_HEAD_DIM, LANES), gq, gk, *tables)


def _mla_prep_body(prompt, zqa_ref, zckv_ref, zkpe_ref, gqa_ref, wuq_ref,
                   gqn_ref, gqpe_ref, gckv_ref, gkpe_ref, cos_ref, sa_ref, sb_ref, wx_ref,
                   ckv_ref, kpe_ref, *outs):
    cos, sa, sb = cos_ref[...], sa_ref[...], sb_ref[...]
    gqn, gqpe = gqn_ref[...], gqpe_ref[...]

    xa = zqa_ref[...]
    qa = (xa * lax.rsqrt(jnp.mean(xa * xa, axis=-1, keepdims=True) + EPS) * gqa_ref[...]).astype(BF16)
    xc = zckv_ref[...]
    ckv = xc * lax.rsqrt(jnp.mean(xc * xc, axis=-1, keepdims=True) + EPS) * gckv_ref[...]
    xk = zkpe_ref[...]
    kpe = _rope(xk * lax.rsqrt(_lane_mean_sq(xk, MLA_ROPE) + EPS) * gkpe_ref[...], cos, sa, sb)
    ckv_ref[...] = ckv
    kpe_ref[...] = kpe[:, :MLA_ROPE]

    for h in range(MLA_HEADS):
        qm = jnp.dot(qa, wuq_ref[:, h * MLA_QK:(h + 1) * MLA_QK], preferred_element_type=F32)
        qn = qm[:, :LANES]
        qn = qn * lax.rsqrt(_lane_mean_sq(qn, MLA_NOPE) + EPS) * gqn
        qp = qm[:, LANES:]
        qp = _rope(qp * lax.rsqrt(_lane_mean_sq(qp, MLA_ROPE) + EPS) * gqpe, cos, sa, sb)
        if prompt:
            q_ref = outs[0]
            q_ref[:, h * MLA_QK:h * MLA_QK + LANES] = (qn * (MLA_SCALE * LOG2E)).astype(BF16)
            q_ref[:, h * MLA_QK + LANES:(h + 1) * MLA_QK] = (qp * (MLA_SCALE * LOG2E)).astype(BF16)
        else:
            qlat_ref, qpe_ref = outs
            qlat = jnp.dot(qn.astype(BF16), wx_ref[h], preferred_element_type=F32)
            qlat_ref[:, h * MLA_KV_RANK:(h + 1) * MLA_KV_RANK] = (qlat * MLA_SCALE).astype(BF16)
            qpe_ref[:, h * LANES:(h + 1) * LANES] = (qp * MLA_SCALE).astype(BF16)

    if prompt:
        _, k_ref, v_ref = outs
        kv = jnp.dot(ckv.astype(BF16), wx_ref[...], preferred_element_type=F32)
        kpb = kpe.astype(BF16)
        for h in range(MLA_HEADS):
            k_ref[:, h * MLA_QK:h * MLA_QK + LANES] = kv[:, h * LANES:(h + 1) * LANES].astype(BF16)
            k_ref[:, h * MLA_QK + LANES:(h + 1) * MLA_QK] = kpb
        v_ref[...] = kv[:, MLA_HEADS * MLA_NOPE:].astype(BF16)


def _mla_prep(z, prompt, gqa, wuq, gqn, gqpe, gckv, gkpe, tables, wx, tm, n_pos_blocks):
    T = z.shape[0]
    row = lambda i: (i, 0)
    const = lambda i: (0, 0)
    tab = lambda i: (i % n_pos_blocks, 0)
    once = pl.Buffered(1)
    if prompt:
        wx_spec = pl.BlockSpec(wx.shape, const, pipeline_mode=once)
        extra_specs = [pl.BlockSpec((tm, MLA_HEADS * MLA_QK), row),
                       pl.BlockSpec((tm, MLA_HEADS * MLA_QK), row),
                       pl.BlockSpec((tm, MLA_WIDTH), row)]
        extra_shapes = [jax.ShapeDtypeStruct((T, MLA_HEADS * MLA_QK), BF16),
                        jax.ShapeDtypeStruct((T, MLA_HEADS * MLA_QK), BF16),
                        jax.ShapeDtypeStruct((T, MLA_WIDTH), BF16)]
    else:
        wx_spec = pl.BlockSpec(wx.shape, lambda i: (0, 0, 0), pipeline_mode=once)
        extra_specs = [pl.BlockSpec((tm, MLA_HEADS * MLA_KV_RANK), row),
                       pl.BlockSpec((tm, MLA_HEADS * LANES), row)]
        extra_shapes = [jax.ShapeDtypeStruct((T, MLA_HEADS * MLA_KV_RANK), BF16),
                        jax.ShapeDtypeStruct((T, MLA_HEADS * LANES), BF16)]
    return pl.pallas_call(
        functools.partial(_mla_prep_body, prompt),
        grid=(T // tm,),
        in_specs=[pl.BlockSpec((tm, MLA_Q_RANK), lambda i: (i, COL_QA // MLA_Q_RANK)),
                  pl.BlockSpec((tm, MLA_KV_RANK), lambda i: (i, COL_CKV // MLA_KV_RANK)),
                  pl.BlockSpec((tm, LANES), lambda i: (i, COL_KPE // LANES)),
                  pl.BlockSpec((1, MLA_Q_RANK), const),
                  pl.BlockSpec(wuq.shape, const, pipeline_mode=once),
                  pl.BlockSpec((1, LANES), const),
                  pl.BlockSpec((1, LANES), const),
                  pl.BlockSpec((1, MLA_KV_RANK), const),
                  pl.BlockSpec((1, LANES), const),
                  pl.BlockSpec((tm, LANES), tab),
                  pl.BlockSpec((tm, LANES), tab),
                  pl.BlockSpec((tm, LANES), tab),
                  wx_spec],
        out_specs=[pl.BlockSpec((tm, MLA_KV_RANK), row),
                   pl.BlockSpec((tm, MLA_ROPE), row)] + extra_specs,
        out_shape=[jax.ShapeDtypeStruct((T, MLA_KV_RANK), F32),
                   jax.ShapeDtypeStruct((T, MLA_ROPE), F32)] + extra_shapes,
        compiler_params=_params(("parallel",), 56),
        name="mla_prep_prompt" if prompt else "mla_prep_sample",
    )(z, z, z, gqa, wuq, gqn, gqpe, gckv, gkpe, *tables, wx)


def _da_lambda(lv):
    e1 = jnp.exp(jnp.sum(lv[0:1] * lv[1:2], axis=-1, keepdims=True))
    e2 = jnp.exp(jnp.sum(lv[2:3] * lv[3:4], axis=-1, keepdims=True))
    return e1 - e2 + LAM_INIT


DEC_NBUF = 3
DEC_PAGES = 16
SUM_ROWS = 8


def _with_sum_rows(vt):
    return jnp.concatenate([vt, jnp.ones(vt.shape[:-2] + (SUM_ROWS, vt.shape[-1]), vt.dtype)], axis=-2)


def _softmax_probs_t(s, m_ref, idx):
    m_all = m_ref[idx]
    ms, ps, corrs = [], [], []
    for j in range(s.shape[1] // LANES):
        cs = slice(j * LANES, (j + 1) * LANES)
        sj = s[:, cs]
        m_prev = m_all[:, cs]
        m_new = jnp.maximum(m_prev, jnp.max(sj, axis=0, keepdims=True))
        corr = jnp.exp2(m_prev - m_new)
        ms.append(m_new)
        ps.append(jnp.exp2(sj - m_new).astype(BF16))
        corrs.append(corr)
    m_ref[idx] = jnp.concatenate(ms, axis=1)
    return jnp.concatenate(ps, axis=1), jnp.concatenate(corrs, axis=1)


def _accumulate_t(acc_ref, idx, v_t, p, corr):
    acc_ref[idx] = acc_ref[idx] * corr + jnp.dot(v_t, p, preferred_element_type=F32)


def _da_attn_body(qt_ref, k_ref, vt_ref, g_ref, lv_ref, gs_ref, o_ref, qbd_ref, m_ref, acc_ref, *, t):
    qi = pl.program_id(2)
    cols = 2 * DA_GROUP * t
    low = lax.broadcasted_iota(jnp.int32, (LANES, 1), 0) < DA_HEAD_DIM
    for r in range(DA_GROUP):
        q = qt_ref[r * LANES:(r + 1) * LANES, :]
        zero = jnp.zeros_like(q)
        qbd_ref[:, (2 * r) * t:(2 * r + 1) * t] = jnp.where(low, q, zero)
        qbd_ref[:, (2 * r + 1) * t:(2 * r + 2) * t] = jnp.where(low, zero, q)
    m_ref[...] = jnp.full(m_ref.shape, NEG_INF, F32)
    acc_ref[...] = jnp.zeros(acc_ref.shape, F32)

    def step(ki, masked):
        k = k_ref[pl.ds(pl.multiple_of(ki * t, t), t), :]
        s = jnp.dot(k, qbd_ref[...], preferred_element_type=F32)
        if masked:
            key = lax.broadcasted_iota(jnp.int32, (t, cols), 0)
            qry = lax.broadcasted_iota(jnp.int32, (t, cols), 1) % t
            s = jnp.where(key <= qry, s, NEG_INF)
        p, corr = _softmax_probs_t(s, m_ref, 0)
        _accumulate_t(acc_ref, 0, vt_ref[0, ki], p, corr)

    def loop_body(ki, carry):
        step(ki, False)
        return carry

    lax.fori_loop(0, qi, loop_body, 0)
    step(qi, True)

    lam = _da_lambda(lv_ref[...])
    gs = gs_ref[...] * (1.0 - LAM_INIT)
    inv_l = 1.0 / acc_ref[0, DA_V_DIM:DA_V_DIM + 1, :]
    for r in range(DA_GROUP):
        a1 = slice((2 * r) * t, (2 * r + 1) * t)
        a2 = slice((2 * r + 1) * t, (2 * r + 2) * t)
        od = (acc_ref[0, :DA_V_DIM, a1] * inv_l[:, a1]
              - lam * (acc_ref[0, :DA_V_DIM, a2] * inv_l[:, a2]))
        y = (od * lax.rsqrt(jnp.mean(od * od, axis=0, keepdims=True) + EPS)).T
        sl = slice(r * LANES, (r + 1) * LANES)
        o_ref[:, sl] = (y * gs * _silu(g_ref[:, sl])).astype(BF16)


def _da_attn(qt, kb, vt, z, lv, gs, B, S, t):
    T = kb.shape[0]
    nq = S // t
    gw = DA_GROUP * LANES
    cols = 2 * DA_GROUP * t
    return pl.pallas_call(
        functools.partial(_da_attn_body, t=t),
        grid=(B, DA_KV_HEADS, nq),
        in_specs=[pl.BlockSpec((gw, t), lambda b, g, i: (g, b * nq + i)),
                  pl.BlockSpec((S, LANES), lambda b, g, i: (b, g)),
                  pl.BlockSpec((1, nq, DA_V_DIM + SUM_ROWS, t), lambda b, g, i: (g, b, 0, 0)),
                  pl.BlockSpec((t, gw), lambda b, g, i: (b * nq + i, COL_GA // gw + g)),
                  pl.BlockSpec((4, DA_HEAD_DIM), lambda b, g, i: (0, 0)),
                  pl.BlockSpec((1, LANES), lambda b, g, i: (0, 0))],
        out_specs=pl.BlockSpec((t, gw), lambda b, g, i: (b * nq + i, g)),
        out_shape=jax.ShapeDtypeStruct((T, DA_WIDTH), BF16),
        scratch_shapes=[pltpu.VMEM((LANES, cols), BF16),
                        pltpu.VMEM((1, 1, cols), F32),
                        pltpu.VMEM((1, DA_V_DIM + SUM_ROWS, cols), F32)],
        compiler_params=_params(("parallel", "parallel", "arbitrary"), 48),
        name="da_attn",
    )(qt, kb, vt, z, lv, gs)


def _mla_attn_body(qt_ref, k_ref, vt_ref, g_ref, o_ref, m_ref, acc_ref, *, t, hb):
    qi = pl.program_id(2)
    m_ref[...] = jnp.full(m_ref.shape, NEG_INF, F32)
    acc_ref[...] = jnp.zeros(acc_ref.shape, F32)

    def step(ki, masked):
        row0 = pl.multiple_of(ki * t, t)
        scores = []
        for h in range(hb):
            k = k_ref[pl.ds(row0, t), h * MLA_QK:(h + 1) * MLA_QK]
            s = jnp.dot(k, qt_ref[h * MLA_QK:(h + 1) * MLA_QK, :], preferred_element_type=F32)
            if masked:
                key = lax.broadcasted_iota(jnp.int32, (t, t), 0)
                qry = lax.broadcasted_iota(jnp.int32, (t, t), 1)
                s = jnp.where(key <= qry, s, NEG_INF)
            scores.append(s)
        probs = [_softmax_probs_t(scores[h], m_ref, h) for h in range(hb)]
        for h in range(hb):
            _accumulate_t(acc_ref, h, vt_ref[h, ki], *probs[h])

    def loop_body(ki, carry):
        step(ki, False)
        return carry

    lax.fori_loop(0, qi, loop_body, 0)
    step(qi, True)

    for h in range(hb):
        sl = slice(h * MLA_V, (h + 1) * MLA_V)
        o = (acc_ref[h, :MLA_V, :] * (1.0 / acc_ref[h, MLA_V:MLA_V + 1, :])).T
        o_ref[:, sl] = (o * _silu(g_ref[:, sl])).astype(BF16)


def _mla_attn(qt, k, vt, z, B, S, t, hb):
    T = k.shape[0]
    nq = S // t
    return pl.pallas_call(
        functools.partial(_mla_attn_body, t=t, hb=hb),
        grid=(B, MLA_HEADS // hb, nq),
        in_specs=[pl.BlockSpec((hb * MLA_QK, t), lambda b, h, i: (h, b * nq + i)),
                  pl.BlockSpec((S, hb * MLA_QK), lambda b, h, i: (b, h)),
                  pl.BlockSpec((hb, nq, MLA_V + SUM_ROWS, t), lambda b, h, i: (h, b, 0, 0)),
                  pl.BlockSpec((t, hb * MLA_V), lambda b, h, i: (b * nq + i, COL_GB // (hb * MLA_V) + h))],
        out_specs=pl.BlockSpec((t, hb * MLA_V), lambda b, h, i: (b * nq + i, h)),
        out_shape=jax.ShapeDtypeStruct((T, MLA_WIDTH), BF16),
        scratch_shapes=[pltpu.VMEM((hb, 1, t), F32),
                        pltpu.VMEM((hb, MLA_V + SUM_ROWS, t), F32)],
        compiler_params=_params(("parallel", "parallel", "arbitrary"), 48),
        name="mla_attn",
    )(qt, k, vt, z)


def _online_update(m_ref, l_ref, acc_ref, s, pv_fn):
    m_prev = m_ref[...]
    m_new = jnp.maximum(m_prev, jnp.max(s, axis=-1, keepdims=True))
    corr = jnp.exp(m_prev - m_new)
    p = jnp.exp(s - m_new)
    l_ref[...] = l_ref[...] * corr + jnp.sum(p, axis=-1, keepdims=True)
    acc_ref[...] = acc_ref[...] * corr + pv_fn(p)
    m_ref[...] = m_new


def _decode_body(pt_ref, qda_ref, kn_ref, vn_ref, qlat_ref, qpe_ref, ckvn_ref, kpen_ref, lv_ref,
                 ck_hbm, kp_hbm, dk_hbm, dv_hbm, oda_ref, olat_ref,
                 ck_buf, kp_buf, dk_buf, dv_buf, sem,
                 m_ml, l_ml, acc_ml, m_da, l_da, acc_da, *, P, NC, page):
    n = pl.program_id(0)
    c = n % NC
    nbuf = ck_buf.shape[0]
    slot = n % nbuf
    nt = (((1,), (1,)), ((), ()))

    def copies(n_, slot_, real):
        b_ = n_ // NC
        c_ = n_ % NC
        out = []
        for j in range(P):
            pg = pt_ref[b_, c_ * P + j] if real else 0
            out.append(pltpu.make_async_copy(
                ck_hbm.at[pg], ck_buf.at[slot_, pl.ds(j * page, page), :], sem.at[0, slot_]))
            out.append(pltpu.make_async_copy(
                kp_hbm.at[pg], kp_buf.at[slot_, :, pl.ds(j * page, page)], sem.at[1, slot_]))
            out.append(pltpu.make_async_copy(
                dk_hbm.at[pg], dk_buf.at[slot_, pl.ds(j * 2 * page, 2 * page), :], sem.at[2, slot_]))
            out.append(pltpu.make_async_copy(
                dv_hbm.at[pg], dv_buf.at[slot_, pl.ds(j * 2 * page, 2 * page), :], sem.at[3, slot_]))
        return out

    @pl.when(n == 0)
    def _():
        for d in range(nbuf - 1):
            for cp in copies(d, d, True):
                cp.start()

    @pl.when(n + nbuf - 1 < pl.num_programs(0))
    def _():
        for cp in copies(n + nbuf - 1, (n + nbuf - 1) % nbuf, True):
            cp.start()

    for cp in copies(n, slot, False):
        cp.wait()

    @pl.when(c == 0)
    def _():
        for m_ref, l_ref, acc_ref in ((m_ml, l_ml, acc_ml), (m_da, l_da, acc_da)):
            m_ref[...] = jnp.full(m_ref.shape, NEG_INF, F32)
            l_ref[...] = jnp.zeros(l_ref.shape, F32)
            acc_ref[...] = jnp.zeros(acc_ref.shape, F32)

    ck = ck_buf[slot].astype(BF16)
    kp = kp_buf[slot].astype(BF16)
    ql = qlat_ref[0]
    qp = qpe_ref[0][:, :MLA_ROPE]
    s = (lax.dot_general(ql, ck, nt, preferred_element_type=F32)
         + jnp.dot(qp, kp, preferred_element_type=F32))
    _online_update(m_ml, l_ml, acc_ml, s,
                   lambda p: jnp.dot(p.astype(BF16), ck, preferred_element_type=F32))

    low = lax.broadcasted_iota(jnp.int32, (1, LANES), 1) < DA_HEAD_DIM
    qda = qda_ref[0].astype(F32)
    parts = []
    for g in range(DA_KV_HEADS):
        q = qda[g * DA_GROUP:(g + 1) * DA_GROUP, :]
        zero = jnp.zeros_like(q)
        parts += [jnp.where(low, q, zero), jnp.where(low, zero, q)]
    q32 = jnp.concatenate(parts, axis=0)
    dk = dk_buf[slot].astype(BF16)
    dv = dv_buf[slot].astype(BF16)
    s = lax.dot_general(q32.astype(BF16), dk, nt, preferred_element_type=F32)
    col_head = lax.broadcasted_iota(jnp.int32, s.shape, 1) % DA_KV_HEADS
    row_head = lax.broadcasted_iota(jnp.int32, s.shape, 0) // (2 * DA_GROUP)
    s = jnp.where(col_head == row_head, s, NEG_INF)
    _online_update(m_da, l_da, acc_da, s,
                   lambda p: jnp.dot(p.astype(BF16), dv, preferred_element_type=F32))

    @pl.when(c == NC - 1)
    def _():
        ckvn = ckvn_ref[0]
        s = (jnp.sum(ql.astype(F32) * ckvn, axis=-1, keepdims=True)
             + jnp.sum(qp.astype(F32) * kpen_ref[0], axis=-1, keepdims=True))
        _online_update(m_ml, l_ml, acc_ml, s, lambda p: p * ckvn)
        olat_ref[0] = acc_ml[...] / l_ml[...]

        per_head = lambda x: jnp.concatenate(
            [jnp.broadcast_to(x[:, g * LANES:(g + 1) * LANES], (2 * DA_GROUP, LANES))
             for g in range(DA_KV_HEADS)], axis=0)
        kn = per_head(kn_ref[0])
        vn = per_head(vn_ref[0])
        s = jnp.sum(q32 * kn, axis=-1, keepdims=True)
        _online_update(m_da, l_da, acc_da, s, lambda p: p * vn)
        o = acc_da[...] / l_da[...]
        lam = _da_lambda(lv_ref[...])
        for g in range(DA_KV_HEADS):
            r0 = g * 2 * DA_GROUP
            oda_ref[0, g * DA_GROUP:(g + 1) * DA_GROUP, :] = (
                o[r0:r0 + DA_GROUP] - lam * o[r0 + DA_GROUP:r0 + 2 * DA_GROUP])


def _decode(page_table, qda, kn, vn, qlat, qpe, ckvn, kpen, lv, ck, kp, dk, dv, P):
    Bd, n_pages = page_table.shape
    page = ck.shape[1]
    NC = n_pages // P
    rows_da = 2 * DA_KV_HEADS * DA_GROUP
    blk = lambda n, pt: (n // NC, 0, 0)
    any_spec = pl.BlockSpec(memory_space=pl.ANY)
    grid_spec = pltpu.PrefetchScalarGridSpec(
        num_scalar_prefetch=1,
        grid=(Bd * NC,),
        in_specs=[pl.BlockSpec((1, DA_HEADS, LANES), blk),
                  pl.BlockSpec((1, 1, DA_KV_HEADS * LANES), blk),
                  pl.BlockSpec((1, 1, DA_KV_HEADS * LANES), blk),
                  pl.BlockSpec((1, MLA_HEADS, MLA_KV_RANK), blk),
                  pl.BlockSpec((1, MLA_HEADS, LANES), blk),
                  pl.BlockSpec((1, 1, MLA_KV_RANK), blk),
                  pl.BlockSpec((1, 1, MLA_ROPE), blk),
                  pl.BlockSpec((4, DA_HEAD_DIM), lambda n, pt: (0, 0)),
                  any_spec, any_spec, any_spec, any_spec],
        out_specs=[pl.BlockSpec((1, DA_HEADS, LANES), blk),
                   pl.BlockSpec((1, MLA_HEADS, MLA_KV_RANK), blk)],
        scratch_shapes=[pltpu.VMEM((DEC_NBUF, P * page, MLA_KV_RANK), F32),
                        pltpu.VMEM((DEC_NBUF, MLA_ROPE, P * page), F32),
                        pltpu.VMEM((DEC_NBUF, P * page * DA_KV_HEADS, LANES), F32),
                        pltpu.VMEM((DEC_NBUF, P * page * DA_KV_HEADS, LANES), F32),
                        pltpu.SemaphoreType.DMA((4, DEC_NBUF)),
                        pltpu.VMEM((MLA_HEADS, 1), F32),
                        pltpu.VMEM((MLA_HEADS, 1), F32),
                        pltpu.VMEM((MLA_HEADS, MLA_KV_RANK), F32),
                        pltpu.VMEM((rows_da, 1), F32),
                        pltpu.VMEM((rows_da, 1), F32),
                        pltpu.VMEM((rows_da, LANES), F32)])
    return pl.pallas_call(
        functools.partial(_decode_body, P=P, NC=NC, page=page),
        grid_spec=grid_spec,
        out_shape=[jax.ShapeDtypeStruct((Bd, DA_HEADS, LANES), F32),
                   jax.ShapeDtypeStruct((Bd, MLA_HEADS, MLA_KV_RANK), F32)],
        compiler_params=_params(("arbitrary",), 56),
        name="decode",
    )(page_table, qda, kn, vn, qlat, qpe, ckvn, kpen, lv, ck, kp, dk, dv)


def _sample_post_body(oda_ref, olat_ref, ga_ref, gb_ref, gs_ref, wuv_ref, oa_ref, ob_ref):
    gs = gs_ref[...]
    for h in range(DA_HEADS):
        sl = slice(h * LANES, (h + 1) * LANES)
        od = oda_ref[:, sl]
        y = od * lax.rsqrt(jnp.mean(od * od, axis=-1, keepdims=True) + EPS) * gs * (1.0 - LAM_INIT)
        oa_ref[:, sl] = (y * _silu(ga_ref[:, sl])).astype(BF16)
    for h in range(MLA_HEADS):
        sl = slice(h * MLA_V, (h + 1) * MLA_V)
        ol = olat_ref[:, h * MLA_KV_RANK:(h + 1) * MLA_KV_RANK].astype(BF16)
        ob = jnp.dot(ol, wuv_ref[h], preferred_element_type=F32)
        ob_ref[:, sl] = (ob * _silu(gb_ref[:, sl])).astype(BF16)


def _sample_post(oda, olat, z, gs, wuv):
    Bd = oda.shape[0]
    const = lambda i: (0, 0)
    return pl.pallas_call(
        _sample_post_body,
        grid=(1,),
        in_specs=[pl.BlockSpec((Bd, DA_WIDTH), const),
                  pl.BlockSpec((Bd, MLA_HEADS * MLA_KV_RANK), const),
                  pl.BlockSpec((Bd, DA_WIDTH), lambda i: (0, COL_GA // DA_WIDTH)),
                  pl.BlockSpec((Bd, MLA_WIDTH), lambda i: (0, COL_GB // MLA_WIDTH)),
                  pl.BlockSpec((1, LANES), const),
                  pl.BlockSpec(wuv.shape, lambda i: (0, 0, 0))],
        out_specs=[pl.BlockSpec((Bd, DA_WIDTH), const),
                   pl.BlockSpec((Bd, MLA_WIDTH), const)],
        out_shape=[jax.ShapeDtypeStruct((Bd, DA_WIDTH), BF16),
                   jax.ShapeDtypeStruct((Bd, MLA_WIDTH), BF16)],
        compiler_params=_params(("arbitrary",), 40),
        name="sample_post",
    )(oda, olat, z, z, gs, wuv)


def _outproj_body(oa_ref, ob_ref, w_ref, x_ref, y_ref):
    ka = oa_ref.shape[1]
    y_ref[...] = (x_ref[...]
                  + jnp.dot(oa_ref[...], w_ref[:ka, :], preferred_element_type=F32)
                  + jnp.dot(ob_ref[...], w_ref[ka:, :], preferred_element_type=F32))


def _outproj(oa, ob, w, x, tm, tn):
    T, D = x.shape
    return pl.pallas_call(
        _outproj_body,
        grid=(T // tm, D // tn),
        in_specs=[pl.BlockSpec((tm, oa.shape[1]), lambda i, j: (i, 0)),
                  pl.BlockSpec((tm, ob.shape[1]), lambda i, j: (i, 0)),
                  pl.BlockSpec((w.shape[0], tn), lambda i, j: (0, j)),
                  pl.BlockSpec((tm, tn), lambda i, j: (i, j))],
        out_specs=pl.BlockSpec((tm, tn), lambda i, j: (i, j)),
        out_shape=jax.ShapeDtypeStruct((T, D), F32),
        compiler_params=_params(("parallel", "arbitrary"), 48),
        name="outproj",
    )(oa, ob, w, x)


def _row_tile(T, cap):
    t = min(T, cap)
    while T % t:
        t //= 2
    return t


def kernel(x_prompt, x_sample, cache_diff_k, cache_diff_v, cache_mla_ckv, cache_mla_kpe, page_table,
           g_pre, w_in, da_q_norm, da_k_norm, da_lambda_q1, da_lambda_k1, da_lambda_q2, da_lambda_k2,
           da_subln, mla_q_a_norm, w_uq, mla_qn_norm, mla_qpe_norm, mla_kv_a_norm, mla_kpe_norm,
           w_uk, w_uv, w_out):
    B, S, D = x_prompt.shape
    Bd, Td, _ = x_sample.shape
    depth, n_phys, page = cache_mla_ckv.shape[:3]
    n_pages = page_table.shape[1]
    assert depth == 1 and Td == 1
    assert w_uq.shape[1:] == (MLA_Q_RANK, MLA_HEADS, MLA_NOPE + MLA_ROPE)
    assert w_uk.shape[1:] == (MLA_KV_RANK, MLA_HEADS, MLA_NOPE)
    assert w_uv.shape[1:] == (MLA_KV_RANK, MLA_HEADS, MLA_V)
    assert cache_diff_k.shape[3:] == (DA_KV_HEADS, 2 * DA_HEAD_DIM)
    assert w_in.shape[2] == COL_KPE + MLA_ROPE

    w = w_in[0].T
    sizes = (DA_HEADS * 2 * DA_HEAD_DIM, DA_KV_HEADS * 2 * DA_HEAD_DIM, DA_KV_HEADS * DA_V_DIM, DA_WIDTH,
             MLA_Q_RANK, MLA_KV_RANK, MLA_ROPE, MLA_WIDTH)
    offs = [0]
    for s_ in sizes:
        offs.append(offs[-1] + s_)
    sec = [w[offs[i]:offs[i + 1]] for i in range(len(sizes))]
    zq, zk, zv, ga, zqa, zckv, zkpe, gb = sec
    w_perm = jnp.concatenate(
        [s_.astype(BF16) for s_ in (zq, ga, gb, zqa, zckv, zk, zv, zkpe)]
        + [jnp.zeros((N_PAD - COL_KPE - MLA_ROPE, D), BF16)], axis=0)
    wuq = w_uq[0]
    wuq_pad = jnp.concatenate(
        [wuq, jnp.zeros((MLA_Q_RANK, MLA_HEADS, MLA_QK - MLA_NOPE - MLA_ROPE), wuq.dtype)], axis=-1)
    wuq_pad = wuq_pad.reshape(MLA_Q_RANK, MLA_HEADS * MLA_QK).astype(BF16)
    wkv = jnp.concatenate([w_uk[0].reshape(MLA_KV_RANK, -1), w_uv[0].reshape(MLA_KV_RANK, -1)],
                          axis=1).astype(BF16)
    wuk_t = jnp.transpose(w_uk[0], (1, 2, 0)).astype(BF16)
    wuv_h = jnp.transpose(w_uv[0], (1, 0, 2)).astype(BF16)
    w_o = w_out[0].astype(BF16)

    tile2 = lambda g: jnp.concatenate([g, g], axis=-1)
    pad2 = lambda g: jnp.concatenate([g, jnp.zeros_like(g)], axis=-1)
    gq, gk = tile2(da_q_norm), tile2(da_k_norm)
    gqpe, gkpe = pad2(mla_qpe_norm), pad2(mla_kpe_norm)
    lv = jnp.concatenate([da_lambda_q1, da_lambda_k1, da_lambda_q2, da_lambda_k2], axis=0)

    Tp = B * S
    xp = x_prompt.reshape(Tp, D)
    tm = _row_tile(S, 512)
    pos_p = jnp.arange(S)
    z = _inproj(xp, g_pre, w_perm, tm)
    q_da, k_da, v_da, kb_da, vb_da = _da_prep(z, gq, gk, _rope_tables(pos_p, True), tm, S // tm,
                                              DA_SCALE * LOG2E)
    tm_ml = _row_tile(S, 512)
    ckv_p, kpe_p, q_ml, k_ml, v_ml = _mla_prep(
        z, True, mla_q_a_norm, wuq_pad, mla_qn_norm, gqpe, mla_kv_a_norm, gkpe,
        _rope_tables(pos_p, False), wkv, tm_ml, S // tm_ml)
    t_da = _row_tile(S, 256)
    t_ml = _row_tile(S, 512)
    key_tiles_t = lambda v, t, heads, dim: _with_sum_rows(jnp.transpose(
        v.reshape(Tp // t, t, heads, dim), (2, 0, 3, 1)))
    o_a = _da_attn(q_da.T, kb_da, key_tiles_t(vb_da, t_da, DA_KV_HEADS, DA_V_DIM), z, lv, da_subln,
                   B, S, t_da)
    o_b = _mla_attn(q_ml.T, k_ml, key_tiles_t(v_ml, t_ml, MLA_HEADS, MLA_V), z, B, S, t_ml, 4)
    y_p = _outproj(o_a, o_b, w_o, xp, _row_tile(Tp, 1024), 512).reshape(B, S, D)

    xs = x_sample.reshape(Bd, D)
    pos_s = jnp.full((Bd,), n_pages * page, jnp.int32)
    zs = _inproj(xs, g_pre, w_perm, Bd)
    qs_da, ks_da, vs_da, _, _ = _da_prep(zs, gq, gk, _rope_tables(pos_s, True), Bd, 1, DA_SCALE)
    ckv_s, kpe_s, qlat, qpe = _mla_prep(
        zs, False, mla_q_a_norm, wuq_pad, mla_qn_norm, gqpe, mla_kv_a_norm, gkpe,
        _rope_tables(pos_s, False), wuk_t, Bd, 1)
    kvw = DA_KV_HEADS * LANES
    oda, olat = _decode(
        page_table,
        qs_da.reshape(Bd, DA_HEADS, LANES), ks_da.reshape(Bd, 1, kvw), vs_da.reshape(Bd, 1, kvw),
        qlat.reshape(Bd, MLA_HEADS, MLA_KV_RANK), qpe.reshape(Bd, MLA_HEADS, LANES),
        ckv_s.reshape(Bd, 1, MLA_KV_RANK), kpe_s.reshape(Bd, 1, MLA_ROPE), lv,
        cache_mla_ckv.reshape(n_phys, page, MLA_KV_RANK),
        jnp.swapaxes(cache_mla_kpe.reshape(n_phys, page, MLA_ROPE), 1, 2),
        cache_diff_k.reshape(n_phys, page * DA_KV_HEADS, LANES),
        cache_diff_v.reshape(n_phys, page * DA_KV_HEADS, LANES),
        _row_tile(n_pages, DEC_PAGES))
    oa_s, ob_s = _sample_post(oda.reshape(Bd, DA_WIDTH), olat.reshape(Bd, MLA_HEADS * MLA_KV_RANK),
                              zs, da_subln, wuv_h)
    y_s = _outproj(oa_s, ob_s, w_o, xs, Bd, 512).reshape(Bd, 1, D)

    return (y_p, y_s,
            k_da.reshape(1, B, S, DA_KV_HEADS, 2 * DA_HEAD_DIM), v_da.reshape(1, B, S, DA_KV_HEADS, DA_V_DIM),
            ckv_p.reshape(1, B, S, MLA_KV_RANK), kpe_p.reshape(1, B, S, MLA_ROPE),
            ks_da.reshape(1, Bd, 1, DA_KV_HEADS, 2 * DA_HEAD_DIM), vs_da.reshape(1, Bd, 1, DA_KV_HEADS, DA_V_DIM),
            ckv_s.reshape(1, Bd, 1, MLA_KV_RANK), kpe_s.reshape(1, Bd, 1, MLA_ROPE))
```
